```python
import math
import jax
import jax.numpy as jnp
from jax import lax
import numpy as np

D_MODEL = 2048
BATCH = 4
SEQ = 2048
DEPTH = 2
DEC_BATCH = 128
DEC_SEQ = 1
PAST_LEN = 2048
PAGE_SIZE = 128

HEAD_DIM = 128
N_HEADS = D_MODEL // HEAD_DIM
H_A = (3 * N_HEADS) // 8
H_B = N_HEADS // 4
H_C = N_HEADS - H_A - H_B
MIX_WIDTH = N_HEADS * HEAD_DIM
KVH_A = 2
IDX_HEADS = 16
IDX_DIM = 64
TOPK_MAX = 256
HALF_DIM = HEAD_DIM // 2
SUBLN_EPS = 1e-5
KVH_C = 2
HPG_C = H_C // KVH_C
CMP_BLOCK = 64
SEL_BLOCKS = 8
WINDOW = 512
ROPE_THETA = 10000.0
NORM_EPS = 1e-6
Q_BLOCK = 128
F32 = jnp.float32

IN_SPLITS = (
    ('a_q', H_A * HEAD_DIM), ('a_k', KVH_A * HEAD_DIM), ('a_v', KVH_A * HEAD_DIM),
    ('a_qi', IDX_HEADS * IDX_DIM), ('a_ki', IDX_DIM), ('a_wi', IDX_HEADS), ('a_z', H_A * HEAD_DIM),
    ('b_q', H_B * HEAD_DIM), ('b_k', H_B * HEAD_DIM), ('b_v', H_B * HEAD_DIM), ('b_z', H_B * HEAD_DIM),
    ('c_q', H_C * HEAD_DIM), ('c_kc', KVH_C * HEAD_DIM), ('c_vc', KVH_C * HEAD_DIM),
    ('c_ks', KVH_C * HEAD_DIM), ('c_vs', KVH_C * HEAD_DIM), ('c_kw', KVH_C * HEAD_DIM),
    ('c_vw', KVH_C * HEAD_DIM), ('c_g', 3 * H_C), ('c_z', H_C * HEAD_DIM),
)
IN_WIDTH = sum(w for _, w in IN_SPLITS)

kernel_name = 'hymba_dsa_diff_nsa_adaln_step'


def split_in(u):
    parts, off = {}, 0
    for name, width in IN_SPLITS:
        parts[name] = u[..., off:off + width]
        off += width
    return parts


def rms_norm(x, g, eps=NORM_EPS):
    xf = x.astype(F32)
    y = xf * lax.rsqrt(jnp.mean(xf * xf, axis=-1, keepdims=True) + eps)
    return (y * g.astype(F32)).astype(x.dtype)


def rope(x, pos):
    d = x.shape[-1]
    inv = ROPE_THETA ** (-jnp.arange(0, d, 2, dtype=F32) / d)
    ang = pos.astype(F32)[:, None] * inv[None, :]
    cos = jnp.cos(ang)[None, :, None, :]
    sin = jnp.sin(ang)[None, :, None, :]
    xf = x.astype(F32)
    x1, x2 = xf[..., : d // 2], xf[..., d // 2:]
    return jnp.concatenate([x1 * cos - x2 * sin, x1 * sin + x2 * cos], axis=-1).astype(x.dtype)


def masked_softmax(s, mask):
    s = jnp.where(mask, s.astype(F32), -jnp.inf)
    m = jnp.max(s, axis=-1, keepdims=True)
    m = jnp.where(jnp.isfinite(m), m, 0.0)
    e = jnp.exp(s - m)
    return e / jnp.maximum(jnp.sum(e, axis=-1, keepdims=True), 1e-30)


def over_query_blocks(fn, arrays, qpos):
    T = qpos.shape[0]
    qb = Q_BLOCK if T % Q_BLOCK == 0 else T
    nb = T // qb
    blk = tuple(jnp.swapaxes(a.reshape((a.shape[0], nb, qb) + a.shape[2:]), 0, 1) for a in arrays)
    out = lax.map(lambda args: fn(*args), blk + (qpos.reshape(nb, qb),))
    out = jnp.swapaxes(out, 0, 1)
    return out.reshape((out.shape[0], T) + out.shape[3:])


def gather_past(pool, layer, page_table, part=None):
    rows = pool[layer, page_table] if part is None else pool[layer, page_table, :, part]
    b, n_pages = page_table.shape
    return rows.reshape((b, n_pages * PAGE_SIZE) + rows.shape[3:])


def gather_rows(new_rows, pos, n_past, pool, layer, page_table, head=None):
    B, T = new_rows.shape[:2]
    b_idx = jnp.arange(B).reshape((B,) + (1,) * (pos.ndim - 1))
    hsel = () if head is None else (slice(None), head)
    rows_new = new_rows[(b_idx, jnp.clip(pos - n_past, 0, T - 1)) + hsel]
    if pool is None:
        return rows_new
    past = jnp.clip(pos, 0, n_past - 1)
    phys = page_table[b_idx, past // PAGE_SIZE]
    rows_past = pool[(layer, phys, past % PAGE_SIZE) + hsel]
    is_past = (pos < n_past).reshape(pos.shape + (1,) * (rows_new.ndim - pos.ndim))
    return jnp.where(is_past, rows_past, rows_new)


def dsa_mixer(p, pos, n_past, layer, cache_kv, cache_kidx, page_table):
    B, T = p['a_q'].shape[:2]
    q = rope(p['a_q'].reshape(B, T, H_A, HEAD_DIM), pos)
    k = rope(p['a_k'].reshape(B, T, KVH_A, HEAD_DIM), pos)
    v = p['a_v'].reshape(B, T, KVH_A, HEAD_DIM)
    new_kv = jnp.stack([k, v], axis=2)
    qi = rope(p['a_qi'].reshape(B, T, IDX_HEADS, IDX_DIM), pos)
    ki = rope(p['a_ki'].reshape(B, T, 1, IDX_DIM), pos)[:, :, 0]
    wi = p['a_wi'] * IDX_HEADS ** -0.5
    ki_all = ki if cache_kidx is None else jnp.concatenate(
        [gather_past(cache_kidx, layer, page_table), ki], axis=1)
    L = n_past + T
    n_keys = min(TOPK_MAX, L // 4)
    kpos = jnp.arange(L)

    def block(qb_q, qb_qi, qb_w, qpos):
        nq = qpos.shape[0]
        rel = jax.nn.relu(jnp.einsum('bqhd,bsd->bqhs', qb_qi, ki_all, preferred_element_type=F32) * IDX_DIM ** -0.5)
        score = jnp.einsum('bqh,bqhs->bqs', qb_w.astype(F32), rel)
        vis = kpos[None, :] <= qpos[:, None]
        score = jnp.where(vis[None], score, -jnp.inf)
        _, sel = lax.top_k(score, n_keys)
        ok = sel <= qpos[None, :, None]
        rows = gather_rows(new_kv, sel, n_past, cache_kv, layer, page_table)
        qg = qb_q.reshape(B, nq, KVH_A, H_A // KVH_A, HEAD_DIM)
        s = jnp.einsum('bqgnd,bqkgd->bqgnk', qg, rows[:, :, :, 0], preferred_element_type=F32) * HEAD_DIM ** -0.5
        pr = masked_softmax(s, ok[:, :, None, None, :])
        o = jnp.einsum('bqgnk,bqkgd->bqgnd', pr.astype(rows.dtype), rows[:, :, :, 1])
        return o.reshape(B, nq, H_A * HEAD_DIM).astype(qb_q.dtype)

    y = over_query_blocks(block, (q, qi, wi), pos)
    return y, new_kv, ki


def diff_mixer(p, pos, n_past, layer, cache_kv, page_table, lam_p, subln_g):
    B, T = p['b_q'].shape[:2]
    q = p['b_q'].reshape(B, T, H_B, 2, HALF_DIM)
    k = p['b_k'].reshape(B, T, H_B, 2, HALF_DIM)
    q12 = jnp.stack([rope(q[..., 0, :], pos), rope(q[..., 1, :], pos)], axis=3)
    k12 = jnp.stack([rope(k[..., 0, :], pos), rope(k[..., 1, :], pos)], axis=3)
    v = p['b_v'].reshape(B, T, H_B, HEAD_DIM)
    new_kv = jnp.stack([k12.reshape(B, T, H_B, HEAD_DIM), v], axis=2)
    lam_init = 0.8 - 0.6 * math.exp(-0.3 * layer)
    lp = lam_p.astype(F32)
    lam = jnp.exp(jnp.sum(lp[0] * lp[1])) - jnp.exp(jnp.sum(lp[2] * lp[3])) + lam_init
    key_sets = []
    if cache_kv is not None:
        kp = gather_past(cache_kv, layer, page_table, 0).reshape(B, n_past, H_B, 2, HALF_DIM)
        vp = gather_past(cache_kv, layer, page_table, 1)
        key_sets.append((jnp.arange(n_past), kp, vp))
    key_sets.append((n_past + jnp.arange(T), k12, v))

    def block(qb, qpos):
        nq = qpos.shape[0]
        s = jnp.concatenate([jnp.einsum('bqhcd,bshcd->bchqs', qb, kk, preferred_element_type=F32)
                             for _, kk, _ in key_sets], axis=-1) * HALF_DIM ** -0.5
        vis = jnp.concatenate([kp_[None, :] <= qpos[:, None] for kp_, _, _ in key_sets], axis=-1)
        pr = masked_softmax(s, vis)
        a = pr[:, 0] - lam * pr[:, 1]
        o, off = 0.0, 0
        for kp_, _, vv in key_sets:
            n = kp_.shape[0]
            o = o + jnp.einsum('bhqs,bshd->bqhd', a[..., off:off + n].astype(vv.dtype), vv)
            off += n
        o = rms_norm(o, subln_g, SUBLN_EPS) * (1.0 - lam_init)
        return o.reshape(B, nq, H_B * HEAD_DIM).astype(qb.dtype)

    y = over_query_blocks(block, (q12,), pos)
    return y, new_kv


def nsa_mixer(p, pos, n_past, layer, cache_cmp, cache_slc, win_buf, page_table, alpha, pe):
    B, T = p['c_q'].shape[:2]
    q = rope(p['c_q'].reshape(B, T, H_C, HEAD_DIM), pos)

    def kv(kname, vname):
        kk = rope(p[kname].reshape(B, T, KVH_C, HEAD_DIM), pos)
        return jnp.stack([kk, p[vname].reshape(B, T, KVH_C, HEAD_DIM)], axis=2)

    new_cmp, new_slc, new_win = kv('c_kc', 'c_vc'), kv('c_ks', 'c_vs'), kv('c_kw', 'c_vw')
    gates = jax.nn.sigmoid(p['c_g'].astype(F32)).reshape(B, T, H_C, 3)
    pe_term = jnp.einsum('lrgd,lrg->rgd', pe, alpha)

    def compress(rows):
        nb = rows.shape[1] // CMP_BLOCK
        r = rows[:, :nb * CMP_BLOCK].reshape(B, nb, CMP_BLOCK, 2, KVH_C, HEAD_DIM)
        return jnp.einsum('bjlrgd,lrg->bjrgd', r, alpha) + pe_term

    parts = []
    if cache_cmp is not None:
        parts.append(compress(gather_past(cache_cmp, layer, page_table)))
    if T // CMP_BLOCK > 0:
        parts.append(compress(new_cmp))
    cmp = parts[0] if len(parts) == 1 else jnp.concatenate(parts, axis=1)
    n_cmp = cmp.shape[1]
    cmp_end = (jnp.arange(n_cmp) + 1) * CMP_BLOCK - 1
    L = n_past + T
    n_blocks = -(-L // CMP_BLOCK)
    n_sel = min(SEL_BLOCKS, n_blocks)
    blk_ids = jnp.arange(n_blocks)
    if win_buf is None:
        win_seq = jnp.concatenate([jnp.zeros((B, WINDOW) + new_win.shape[2:], new_win.dtype), new_win], axis=1)
        keep = min(WINDOW, T)
    else:
        pad = jnp.zeros((B, WINDOW - win_buf.shape[1]) + new_win.shape[2:], new_win.dtype)
        win_seq = jnp.concatenate([pad, win_buf.astype(new_win.dtype), new_win], axis=1)
        keep = win_buf.shape[1]
    win_base = n_past - WINDOW
    scale = HEAD_DIM ** -0.5

    def block(qb_q, qb_g, qpos):
        nq = qpos.shape[0]
        qg = qb_q.reshape(B, nq, KVH_C, HPG_C, HEAD_DIM)
        s = jnp.einsum('bqgnd,bjgd->bqgnj', qg, cmp[:, :, 0], preferred_element_type=F32) * scale
        vis = cmp_end[None, :] <= qpos[:, None]
        pc = masked_softmax(s, vis[None, :, None, None, :])
        o_cmp = jnp.einsum('bqgnj,bjgd->bqgnd', pc.astype(cmp.dtype), cmp[:, :, 1])
        imp = jnp.sum(pc, axis=3)
        if n_cmp < n_blocks:
            imp = jnp.pad(imp, ((0, 0), (0, 0), (0, 0), (0, n_blocks - n_cmp)))
        cur = qpos // CMP_BLOCK
        forced = (blk_ids[None, :] == 0) | (blk_ids[None, :] == cur[:, None]) | (blk_ids[None, :] == cur[:, None] - 1)
        allowed = blk_ids[None, :] <= cur[:, None]
        imp = jnp.where(forced[None, :, None, :], jnp.inf, imp)
        imp = jnp.where(allowed[None, :, None, :], imp, -jnp.inf)
        _, sel = lax.top_k(imp, n_sel)
        tok = sel[..., None] * CMP_BLOCK + jnp.arange(CMP_BLOCK)
        ok = (tok <= qpos[None, :, None, None, None]).reshape(B, nq, KVH_C, n_sel * CMP_BLOCK)
        head = jnp.arange(KVH_C).reshape(1, 1, KVH_C, 1, 1)
        rows = gather_rows(new_slc, tok, n_past, cache_slc, layer, page_table, head=head)
        rows = rows.reshape(B, nq, KVH_C, n_sel * CMP_BLOCK, 2, HEAD_DIM)
        s = jnp.einsum('bqgnd,bqgkd->bqgnk', qg, rows[..., 0, :], preferred_element_type=F32) * scale
        ps = masked_softmax(s, ok[:, :, :, None, :])
        o_slc = jnp.einsum('bqgnk,bqgkd->bqgnd', ps.astype(rows.dtype), rows[..., 1, :])
        start = qpos[0] - n_past
        wk = lax.dynamic_slice_in_dim(win_seq, start, nq + WINDOW, axis=1)
        wpos = win_base + start + jnp.arange(nq + WINDOW)
        wvis = (wpos[None, :] <= qpos[:, None]) & (wpos[None, :] > qpos[:, None] - WINDOW) & (wpos[None, :] >= 0)
        s = jnp.einsum('bqgnd,bkgd->bqgnk', qg, wk[:, :, 0], preferred_element_type=F32) * scale
        pw = masked_softmax(s, wvis[None, :, None, None, :])
        o_win = jnp.einsum('bqgnk,bkgd->bqgnd', pw.astype(wk.dtype), wk[:, :, 1])
        o = jnp.stack([o_cmp, o_slc, o_win], axis=-1)
        gh = qb_g.reshape(B, nq, KVH_C, HPG_C, 1, 3)
        o = jnp.sum(o * gh, axis=-1)
        return o.reshape(B, nq, H_C * HEAD_DIM).astype(qb_q.dtype)

    y = over_query_blocks(block, (q, gates), pos)
    return y, new_cmp, new_slc, win_seq[:, -keep:]


def mixer_layer(x, c, pos, n_past, layer, caches, page_table, prm):
    mod = jnp.dot(jax.nn.silu(c), prm['w_mod'][layer]) + prm['b_mod'][layer]
    shift, scale, gate = jnp.split(mod, 3, axis=-1)
    h = rms_norm(x, prm['norm_g'][layer]) * (1.0 + scale[:, None, :]) + shift[:, None, :]
    p = split_in(jnp.einsum('btd,de->bte', h, prm['w_in'][layer]))
    if caches is None:
        c_akv = c_aki = c_bkv = c_cmp = c_slc = c_win = None
    else:
        c_akv, c_aki, c_bkv, c_cmp, c_slc, win_all = caches
        c_win = win_all[layer]
    ya, a_kv, a_ki = dsa_mixer(p, pos, n_past, layer, c_akv, c_aki, page_table)
    yb, b_kv = diff_mixer(p, pos, n_past, layer, c_bkv, page_table, prm['lam'][layer], prm['subln_g'][layer])
    yc, c_cmp_new, c_slc_new, c_win_new = nsa_mixer(p, pos, n_past, layer, c_cmp, c_slc, c_win, page_table,
                                                     prm['cmp_alpha'][layer], prm['cmp_pe'][layer])
    mixed = jnp.concatenate([ya * jax.nn.silu(p['a_z']), yb * jax.nn.silu(p['b_z']), yc * jax.nn.silu(p['c_z'])], axis=-1)
    out = jnp.einsum('bte,ed->btd', mixed, prm['w_out'][layer])
    return x + gate[:, None, :] * out, (a_kv, a_ki, b_kv, c_cmp_new, c_slc_new, c_win_new)


def run_group(x, c, n_past, caches, page_table, prm):
    T = x.shape[1]
    pos = n_past + jnp.arange(T, dtype=jnp.int32)
    states = []
    for layer in range(DEPTH):
        x, st = mixer_layer(x, c, pos, n_past, layer, caches, page_table, prm)
        states.append(st)
    y = rms_norm(x, prm['final_g'])
    stacked = tuple(jnp.stack([st[i] for st in states]) for i in range(len(states[0])))
    return y, stacked


def setup_inputs(seed: int = 0) -> dict:
    key = jax.random.key(seed)
    ks = jax.random.split(key, 24)
    n_pages = PAST_LEN // PAGE_SIZE
    n_pool = (DEC_BATCH * n_pages * 5) // 4
    win_buf = min(WINDOW, PAST_LEN)

    def nrm(k, shape, s=1.0):
        return s * jax.random.normal(k, shape, F32)

    page_table = jax.random.permutation(ks[0], n_pool)[: DEC_BATCH * n_pages].reshape(DEC_BATCH, n_pages).astype(jnp.int32)
    return {
        'x_prompt': nrm(ks[1], (BATCH, SEQ, D_MODEL)),
        'x_sample': nrm(ks[2], (DEC_BATCH, DEC_SEQ, D_MODEL)),
        'cache_a_kv': nrm(ks[3], (DEPTH, n_pool, PAGE_SIZE, 2, KVH_A, HEAD_DIM)),
        'cache_a_kidx': nrm(ks[4], (DEPTH, n_pool, PAGE_SIZE, IDX_DIM)),
        'cache_b_kv': nrm(ks[5], (DEPTH, n_pool, PAGE_SIZE, 2, H_B, HEAD_DIM)),
        'cache_c_cmp_kv': nrm(ks[6], (DEPTH, n_pool, PAGE_SIZE, 2, KVH_C, HEAD_DIM)),
        'cache_c_slc_kv': nrm(ks[7], (DEPTH, n_pool, PAGE_SIZE, 2, KVH_C, HEAD_DIM)),
        'state_c_win_kv': nrm(ks[8], (DEPTH, DEC_BATCH, win_buf, 2, KVH_C, HEAD_DIM)),
        'page_table': page_table,
        'c_prompt': nrm(ks[9], (BATCH, D_MODEL)),
        'c_sample': nrm(ks[10], (DEC_BATCH, D_MODEL)),
        'w_mod': nrm(ks[11], (DEPTH, D_MODEL, 3 * D_MODEL), 0.5 * D_MODEL ** -0.5),
        'b_mod': nrm(ks[12], (DEPTH, 3 * D_MODEL), 0.02),
        'norm_g': 1.0 + nrm(ks[13], (DEPTH, D_MODEL), 0.02),
        'w_in': nrm(ks[14], (DEPTH, D_MODEL, IN_WIDTH), D_MODEL ** -0.5),
        'w_out': nrm(ks[15], (DEPTH, MIX_WIDTH, D_MODEL), MIX_WIDTH ** -0.5),
        'lam': nrm(ks[16], (DEPTH, 4, HALF_DIM), 0.1),
        'subln_g': 1.0 + nrm(ks[17], (DEPTH, HEAD_DIM), 0.02),
        'cmp_alpha': (1.0 + nrm(ks[18], (DEPTH, CMP_BLOCK, 2, KVH_C), 0.1)) / CMP_BLOCK,
        'cmp_pe': nrm(ks[19], (DEPTH, CMP_BLOCK, 2, KVH_C, HEAD_DIM), 0.1),
        'final_g': 1.0 + nrm(ks[20], (D_MODEL,), 0.02),
    }


def reference(x_prompt, x_sample, cache_a_kv, cache_a_kidx, cache_b_kv, cache_c_cmp_kv, cache_c_slc_kv,
              state_c_win_kv, page_table, c_prompt, c_sample, w_mod, b_mod, norm_g, w_in, w_out, lam,
              subln_g, cmp_alpha, cmp_pe, final_g):
    prm = {'w_mod': w_mod, 'b_mod': b_mod, 'norm_g': norm_g, 'w_in': w_in, 'w_out': w_out, 'lam': lam,
           'subln_g': subln_g, 'cmp_alpha': cmp_alpha, 'cmp_pe': cmp_pe, 'final_g': final_g}
    y_prompt, st_p = run_group(x_prompt, c_prompt, 0, None, None, prm)
    n_past = page_table.shape[1] * PAGE_SIZE
    caches = (cache_a_kv, cache_a_kidx, cache_b_kv, cache_c_cmp_kv, cache_c_slc_kv, state_c_win_kv)
    y_sample, st_s = run_group(x_sample, c_sample, n_past, caches, page_table, prm)
    a_kv_p, a_kidx_p, b_kv_p, cmp_p, slc_p, win_p = st_p
    a_kv_s, a_kidx_s, b_kv_s, cmp_s, slc_s, win_s = st_s
    return (y_prompt, y_sample, a_kv_p, a_kv_s, a_kidx_p, a_kidx_s, b_kv_p, b_kv_s,
            cmp_p, cmp_s, slc_p, slc_s, win_p, win_s)
```

```python
import functools
import math

import jax
import jax.numpy as jnp
from jax import lax
from jax.experimental import pallas as pl
from jax.experimental.pallas import tpu as pltpu

F32 = jnp.float32
BF16 = jnp.bfloat16
I32 = jnp.int32

HEAD_DIM = 128
H_A, H_B, H_C = 6, 4, 6
KVH_A, KVH_C = 2, 2
HPG_A, HPG_C = H_A // KVH_A, H_C // KVH_C
IDX_HEADS, IDX_DIM = 16, 64
TOPK_MAX = 256
HALF_DIM = 64
SUBLN_EPS = 1e-5
CMP_BLOCK = 64
SEL_BLOCKS = 8
WINDOW = 512
ROPE_THETA = 10000.0
NORM_EPS = 1e-6
PAGE_SIZE = 128

LANES = 128
VMEM_LIMIT = 56 * 1024 * 1024

NEG = -1e30
INT_MIN = -2147483648

_IN_SPLITS = (
    ("a_q", 768), ("a_k", 256), ("a_v", 256), ("a_qi", 1024), ("a_ki", 64), ("a_wi", 16), ("a_z", 768),
    ("b_q", 512), ("b_k", 512), ("b_v", 512), ("b_z", 512),
    ("c_q", 768), ("c_kc", 256), ("c_vc", 256), ("c_ks", 256), ("c_vs", 256), ("c_kw", 256),
    ("c_vw", 256), ("c_g", 18), ("c_z", 768),
)
_PAD_ORDER = (
    ("a_q", 768), ("a_z", 768), ("c_q", 768), ("c_z", 768), ("a_qi", 1024),
    ("b_k", 512), ("b_v", 512), ("b_q", 512), ("b_z", 512),
    ("a_k", 256), ("a_v", 256), ("c_kc", 256), ("c_vc", 256), ("c_ks", 256), ("c_vs", 256),
    ("c_kw", 256), ("c_vw", 256), ("a_ki", 128), ("a_wi", 128), ("c_g", 128), ("_pad", 128),
)
_OFF = {}
_o = 0
for _n, _w in _PAD_ORDER:
    _OFF[_n] = _o
    _o += _w
NP = _o
IN_TN = 512


def _params(sem):
    return pltpu.CompilerParams(dimension_semantics=sem, vmem_limit_bytes=VMEM_LIMIT)


def _nt(a, b):
    return lax.dot_general(a, b, (((1,), (1,)), ((), ())), preferred_element_type=F32)


def _silu(z):
    return z * jax.nn.sigmoid(z)


def _mod_kernel(c_ref, w_ref, b_ref, o_ref):
    c = c_ref[...]
    o_ref[0] = jnp.dot(_silu(c), w_ref[0], preferred_element_type=F32,
                       precision=lax.Precision.HIGHEST) + b_ref[0]


def _mod_call(c_all, w_mod, b_mod):
    depth, d, n3 = w_mod.shape
    bc = c_all.shape[0]
    tn = 768
    return pl.pallas_call(
        _mod_kernel,
        grid=(depth, n3 // tn),
        in_specs=[
            pl.BlockSpec((bc, d), lambda l, j: (0, 0)),
            pl.BlockSpec((1, d, tn), lambda l, j: (l, 0, j)),
            pl.BlockSpec((1, 1, tn), lambda l, j: (l, 0, j)),
        ],
        out_specs=pl.BlockSpec((1, bc, tn), lambda l, j: (l, 0, j)),
        out_shape=jax.ShapeDtypeStruct((depth, bc, n3), F32),
        compiler_params=_params(("parallel", "parallel")),
        name="mod",
    )(c_all, w_mod, b_mod.reshape(depth, 1, n3))


def _inproj_kernel(x_ref, g_ref, sc_ref, sh_ref, w_ref, o_ref, h_ref):
    @pl.when(pl.program_id(1) == 0)
    def _():
        x = x_ref[...]
        y = x * lax.rsqrt(jnp.mean(x * x, axis=-1, keepdims=True) + NORM_EPS) * g_ref[...]
        h_ref[...] = (y * (1.0 + sc_ref[...]) + sh_ref[...]).astype(BF16)

    o_ref[...] = jnp.dot(h_ref[...], w_ref[...], preferred_element_type=F32)


def _inproj_call(x, g, scale, shift, w, rows_per_batch):
    m, d = x.shape
    tm = min(m, 512)
    if rows_per_batch > 1:
        per = rows_per_batch // tm
        mspec = pl.BlockSpec((None, 1, d), lambda i, j: (i // per, 0, 0))
    else:
        mspec = pl.BlockSpec((tm, d), lambda i, j: (i, 0))
    return pl.pallas_call(
        _inproj_kernel,
        grid=(m // tm, NP // IN_TN),
        in_specs=[
            pl.BlockSpec((tm, d), lambda i, j: (i, 0)),
            pl.BlockSpec((1, d), lambda i, j: (0, 0)),
            mspec, mspec,
            pl.BlockSpec((d, IN_TN), lambda i, j: (0, j)),
        ],
        out_specs=pl.BlockSpec((tm, IN_TN), lambda i, j: (i, j)),
        out_shape=jax.ShapeDtypeStruct((m, NP), F32),
        scratch_shapes=[pltpu.VMEM((tm, d), BF16)],
        compiler_params=_params(("parallel", "arbitrary")),
        name="inproj",
    )(x, g, scale, shift, w)


def _rope128(x, cos, sin_s):
    return x * cos + pltpu.roll(x, 64, 1) * sin_s


def _rope64(x, cos, sin_s, low32):
    partner = jnp.where(low32, pltpu.roll(x, 96, 1), pltpu.roll(x, 32, 1))
    return x * cos + partner * sin_s


def _post_kernel(c128_ref, s128_ref, c64_ref, s64_ref,
                 uqa_ref, uqc_ref, uqi_ref, ubkv_ref, ubq_ref, uakv_ref, ucmp_ref, uslc_ref, uwin_ref, uki_ref,
                 qa_ref, qc_ref, qi_ref, qb_ref, bkv_ref, akv_ref, cmp_ref, slc_ref, win_ref, ki_ref,
                 bkvh_ref, akvh_ref, slch_ref, winh_ref, kid_ref):
    c128, s128 = c128_ref[...], s128_ref[...]
    c64, s64 = c64_ref[...], s64_ref[...]
    lane = lax.broadcasted_iota(I32, (1, LANES), 1)
    low32 = (lane % 64) < 32

    def heads128(src, dst, n, off=0):
        for h in range(n):
            sl = slice(off + h * LANES, off + (h + 1) * LANES)
            dst[:, sl] = _rope128(src[:, sl], c128, s128).astype(dst.dtype)

    def heads64(src, dst, n, off=0):
        for h in range(n):
            sl = slice(off + h * LANES, off + (h + 1) * LANES)
            dst[:, sl] = _rope64(src[:, sl], c64, s64, low32).astype(dst.dtype)

    heads128(uqa_ref, qa_ref, H_A)
    heads128(uqc_ref, qc_ref, H_C)
    heads64(uqi_ref, qi_ref, IDX_HEADS // 2)
    heads64(ubq_ref, qb_ref, H_B)
    heads64(ubkv_ref, bkv_ref, H_B)
    bkv_ref[:, 512:] = ubkv_ref[:, 512:]
    heads64(ubkv_ref, bkvh_ref, H_B)
    bkvh_ref[:, 512:] = ubkv_ref[:, 512:].astype(BF16)
    for src, dst, dsth in ((uakv_ref, akv_ref, akvh_ref), (ucmp_ref, cmp_ref, None),
                           (uslc_ref, slc_ref, slch_ref), (uwin_ref, win_ref, winh_ref)):
        heads128(src, dst, 2)
        dst[:, 256:] = src[:, 256:]
        if dsth is not None:
            heads128(src, dsth, 2)
            dsth[:, 256:] = src[:, 256:].astype(BF16)
    ki = _rope64(uki_ref[...], c64, s64, low32)
    ki_ref[...] = ki[:, :IDX_DIM]
    kid_ref[...] = (ki + pltpu.roll(ki, 64, 1)).astype(BF16)


def _post_call(u, tabs, rows_per_batch):
    m = u.shape[0]
    tq = min(m, 256)
    if rows_per_batch > 1:
        per = rows_per_batch // tq
        tspec = pl.BlockSpec((tq, LANES), lambda i: (i % per, 0))
    else:
        tspec = pl.BlockSpec((1, LANES), lambda i: (0, 0))

    def col(name, width):
        blk = _OFF[name] // width
        assert blk * width == _OFF[name]
        return pl.BlockSpec((tq, width), lambda i: (i, blk))

    def row(width):
        return pl.BlockSpec((tq, width), lambda i: (i, 0))

    in_specs = [tspec] * 4 + [
        col("a_q", 768), col("c_q", 768), col("a_qi", 1024), col("b_k", 1024), col("b_q", 512),
        col("a_k", 512), col("c_kc", 512), col("c_ks", 512), col("c_kw", 512), col("a_ki", 128),
    ]
    outs = [(768, F32), (768, F32), (1024, F32), (512, F32), (1024, F32), (512, F32), (512, F32), (512, F32),
            (512, F32), (IDX_DIM, F32), (1024, BF16), (512, BF16), (512, BF16), (512, BF16), (128, BF16)]
    res = pl.pallas_call(
        _post_kernel,
        grid=(m // tq,),
        in_specs=in_specs,
        out_specs=[row(w) for w, _ in outs],
        out_shape=[jax.ShapeDtypeStruct((m, w), dt) for w, dt in outs],
        compiler_params=_params(("parallel",)),
        name="post",
    )(*tabs, *([u] * 10))
    names = ("qa", "qc", "qi", "qb", "b_kv", "a_kv", "cmp", "slc", "win", "a_kidx",
             "b_kv_h", "a_kv_h", "slc_h", "win_h", "kidx_dup")
    return dict(zip(names, res))


def _rope_tables(pos):
    def tab(d):
        inv = ROPE_THETA ** (-jnp.arange(0, d, 2, dtype=F32) / d)
        ang = pos.astype(F32)[:, None] * inv[None, :]
        cos, sin = jnp.cos(ang), jnp.sin(ang)
        reps = LANES // d
        return jnp.tile(jnp.concatenate([cos, cos], -1), (1, reps)), jnp.tile(jnp.concatenate([-sin, sin], -1), (1, reps))

    c128, s128 = tab(HEAD_DIM)
    c64, s64 = tab(IDX_DIM)
    return c128, s128, c64, s64


def _softmax_parts(s, maskf):
    sm = jnp.where(maskf > 0.0, s, NEG)
    m = jnp.max(sm, axis=-1, keepdims=True)
    p = jnp.exp(sm - m) * maskf
    return p, jnp.sum(p, axis=-1, keepdims=True)


def _topk_mask(score, valid, k, key_ref, mask_ref):
    r, l = score.shape
    nchunk = l // LANES
    bits = lax.bitcast_convert_type(score + 0.0, I32)
    key = bits ^ ((bits >> 31) & 0x7FFFFFFF)
    key_ref[...] = jnp.where(valid, key, INT_MIN)
    kf = float(k)

    def count_ge(cand):
        acc = jnp.zeros((r, LANES), F32)
        for c in range(nchunk):
            acc = acc + jnp.where(key_ref[:, c * LANES:(c + 1) * LANES] >= cand, 1.0, 0.0)
        return jnp.sum(acc, axis=-1, keepdims=True)

    def body(it, t_u):
        cand_u = t_u | lax.shift_left(jnp.int32(1), 31 - it)
        cnt = count_ge(cand_u ^ INT_MIN)
        return jnp.where(cnt >= kf, cand_u, t_u)

    t_u = lax.fori_loop(0, 32, body, jnp.zeros((r, 1), I32))
    thr = t_u ^ INT_MIN
    need = kf - _count_gt(key_ref, thr, r, nchunk)
    ri = lax.broadcasted_iota(I32, (LANES, LANES), 0)
    ci = lax.broadcasted_iota(I32, (LANES, LANES), 1)
    tri = jnp.where(ri < ci, 1.0, 0.0).astype(BF16)
    carry = jnp.zeros((r, 1), F32)
    for c in range(nchunk):
        sl = slice(c * LANES, (c + 1) * LANES)
        kc = key_ref[:, sl]
        eq = jnp.where(kc == thr, 1.0, 0.0)
        pre = jnp.dot(eq.astype(BF16), tri, preferred_element_type=F32) + carry
        take = jnp.where(kc > thr, 1.0, jnp.where(pre < need, eq, 0.0))
        mask_ref[:, sl] = jnp.where(kc != INT_MIN, take, 0.0)
        carry = carry + jnp.sum(eq, axis=-1, keepdims=True)


def _count_gt(key_ref, thr, r, nchunk):
    acc = jnp.zeros((r, LANES), F32)
    for c in range(nchunk):
        acc = acc + jnp.where(key_ref[:, c * LANES:(c + 1) * LANES] > thr, 1.0, 0.0)
    return jnp.sum(acc, axis=-1, keepdims=True)


def _top_blocks(imp, nsel):
    lane = lax.broadcasted_iota(I32, imp.shape, 1).astype(F32)
    sel = jnp.zeros(imp.shape, F32)
    for _ in range(nsel):
        m = jnp.max(imp, axis=-1, keepdims=True)
        cand = jnp.where(imp == m, lane, 1e9)
        idx = jnp.min(cand, axis=-1, keepdims=True)
        pick = (lane == idx) & (m > -jnp.inf)
        sel = jnp.where(pick, 1.0, sel)
        imp = jnp.where(pick, -jnp.inf, imp)
    return sel


def _block_importance(imp, cur, nblk):
    blk = lax.broadcasted_iota(I32, imp.shape, 1)
    forced = (blk == 0) | (blk == cur) | (blk == cur - 1)
    imp = jnp.where(forced, jnp.inf, imp)
    return jnp.where((blk <= cur) & (blk < nblk), imp, -jnp.inf)


def _lam_value(lam_ref, layer):
    lp = lam_ref[...]
    a = jnp.sum(lp[0:1] * lp[1:2], axis=-1, keepdims=True)
    b = jnp.sum(lp[2:3] * lp[3:4], axis=-1, keepdims=True)
    lam_init = 0.8 - 0.6 * math.exp(-0.3 * layer)
    return jnp.exp(a) - jnp.exp(b) + lam_init, lam_init


def _subln(o, g, lam_init):
    y = o * lax.rsqrt(jnp.mean(o * o, axis=-1, keepdims=True) + SUBLN_EPS)
    return y * g * (1.0 - lam_init)


def _dsa_prompt_kernel(q_ref, qi_ref, wi_ref, z_ref, kid_ref, kv_ref, o_ref, key_ref, mask_ref, *, n_keys):
    tq = q_ref.shape[0]
    t = kv_ref.shape[1]
    qpos = pl.program_id(1) * tq + lax.broadcasted_iota(I32, (tq, 1), 0)
    kpos = lax.broadcasted_iota(I32, (1, t), 1)
    lane = lax.broadcasted_iota(I32, (1, LANES), 1)
    lo = lane < IDX_DIM
    kd = kid_ref[0]
    w = wi_ref[...] * (IDX_HEADS ** -0.5)
    acc = jnp.zeros((tq, t), F32)
    for pr in range(IDX_HEADS // 2):
        qp = qi_ref[:, pr * LANES:(pr + 1) * LANES]
        for half in range(2):
            qm = jnp.where(lo if half == 0 else jnp.logical_not(lo), qp, 0.0).astype(BF16)
            h = 2 * pr + half
            acc = acc + w[:, h:h + 1] * jnp.maximum(_nt(qm, kd) * (IDX_DIM ** -0.5), 0.0)
    _topk_mask(acc, kpos <= qpos, n_keys, key_ref, mask_ref)
    maskf = mask_ref[...]
    kv = kv_ref[0]
    for g in range(KVH_A):
        k_g = kv[:, g * LANES:(g + 1) * LANES]
        v_g = kv[:, (KVH_A + g) * LANES:(KVH_A + g + 1) * LANES]
        for n in range(HPG_A):
            sl = slice((g * HPG_A + n) * LANES, (g * HPG_A + n + 1) * LANES)
            s = _nt(q_ref[:, sl].astype(BF16), k_g) * (HEAD_DIM ** -0.5)
            p, den = _softmax_parts(s, maskf)
            o = jnp.dot(p.astype(BF16), v_g, preferred_element_type=F32) / jnp.maximum(den, 1e-30)
            o_ref[:, sl] = (o * _silu(z_ref[:, sl])).astype(o_ref.dtype)


def _dsa_prompt_call(u, post, b, t):
    tq = 128
    nq = t // tq
    n_keys = min(TOPK_MAX, t // 4)

    def rows(width, blk=0):
        return pl.BlockSpec((tq, width), lambda bi, i: (bi * nq + i, blk))

    return pl.pallas_call(
        functools.partial(_dsa_prompt_kernel, n_keys=n_keys),
        grid=(b, nq),
        in_specs=[
            rows(768), rows(1024),
            rows(128, _OFF["a_wi"] // 128), rows(768, _OFF["a_z"] // 768),
            pl.BlockSpec((1, t, 128), lambda bi, i: (bi, 0, 0)),
            pl.BlockSpec((1, t, 512), lambda bi, i: (bi, 0, 0)),
        ],
        out_specs=rows(768),
        out_shape=jax.ShapeDtypeStruct((b * t, 768), BF16),
        scratch_shapes=[pltpu.VMEM((tq, t), I32), pltpu.VMEM((tq, t), F32)],
        compiler_params=_params(("parallel", "arbitrary")),
        name="dsa_prompt",
    )(post["qa"], post["qi"], u, u, post["kidx_dup"].reshape(b, t, 128), post["a_kv_h"].reshape(b, t, 512))


def _diff_prompt_kernel(q_ref, z_ref, kv_ref, lam_ref, g_ref, o_ref, *, layer):
    tq = q_ref.shape[0]
    t = kv_ref.shape[1]
    qpos = pl.program_id(1) * tq + lax.broadcasted_iota(I32, (tq, 1), 0)
    kpos = lax.broadcasted_iota(I32, (1, t), 1)
    maskf = jnp.where(kpos <= qpos, 1.0, 0.0)
    lane = lax.broadcasted_iota(I32, (1, LANES), 1)
    lo = lane < HALF_DIM
    lam, lam_init = _lam_value(lam_ref, layer)
    kv = kv_ref[0]
    for h in range(H_B):
        sl = slice(h * LANES, (h + 1) * LANES)
        q = q_ref[:, sl]
        k_h = kv[:, sl]
        v_h = kv[:, H_B * LANES + h * LANES:H_B * LANES + (h + 1) * LANES]
        pr = []
        for half in range(2):
            qm = jnp.where(lo if half == 0 else jnp.logical_not(lo), q, 0.0).astype(BF16)
            p, den = _softmax_parts(_nt(qm, k_h) * (HALF_DIM ** -0.5), maskf)
            pr.append(p / jnp.maximum(den, 1e-30))
        a = pr[0] - lam * pr[1]
        o = jnp.dot(a.astype(BF16), v_h, preferred_element_type=F32)
        o = _subln(o, g_ref[...], lam_init)
        o_ref[:, sl] = (o * _silu(z_ref[:, sl])).astype(o_ref.dtype)


def _diff_prompt_call(u, post, lam_l, subln_l, layer, b, t):
    tq = 128
    nq = t // tq

    def rows(width, blk=0):
        return pl.BlockSpec((tq, width), lambda bi, i: (bi * nq + i, blk))

    return pl.pallas_call(
        functools.partial(_diff_prompt_kernel, layer=layer),
        grid=(b, nq),
        in_specs=[
            rows(512), rows(512, _OFF["b_z"] // 512),
            pl.BlockSpec((1, t, 1024), lambda bi, i: (bi, 0, 0)),
            pl.BlockSpec((4, HALF_DIM), lambda bi, i: (0, 0)),
            pl.BlockSpec((1, HEAD_DIM), lambda bi, i: (0, 0)),
        ],
        out_specs=rows(512),
        out_shape=jax.ShapeDtypeStruct((b * t, 512), BF16),
        compiler_params=_params(("parallel", "arbitrary")),
        name="diff_prompt",
    )(post["qb"], u, post["b_kv_h"].reshape(b, t, 1024), lam_l, subln_l.reshape(1, HEAD_DIM))


def _compress_kernel(rows_ref, a_ref, pe_ref, o_ref):
    nb = o_ref.shape[1]
    a = a_ref[...]
    pe_term = jnp.sum(pe_ref[...] * a, axis=0, keepdims=True)
    for j in range(nb):
        blk = rows_ref[0, j * CMP_BLOCK:(j + 1) * CMP_BLOCK, :]
        o_ref[0, j:j + 1, :] = jnp.sum(blk * a, axis=0, keepdims=True) + pe_term


def _compress_call(cmp_rows, alpha_e, pe_e, b, t):
    nb = t // CMP_BLOCK
    return pl.pallas_call(
        _compress_kernel,
        grid=(b,),
        in_specs=[
            pl.BlockSpec((1, t, 512), lambda bi: (bi, 0, 0)),
            pl.BlockSpec((CMP_BLOCK, 512), lambda bi: (0, 0)),
            pl.BlockSpec((CMP_BLOCK, 512), lambda bi: (0, 0)),
        ],
        out_specs=pl.BlockSpec((1, nb, 512), lambda bi: (bi, 0, 0)),
        out_shape=jax.ShapeDtypeStruct((b, nb, 512), F32),
        compiler_params=_params(("parallel",)),
        name="compress",
    )(cmp_rows.reshape(b, t, 512), alpha_e, pe_e)


def _nsa_prompt_kernel(q_ref, z_ref, gt_ref, cmp_ref, slc_ref, win_ref, o_ref):
    tq = q_ref.shape[0]
    t = slc_ref.shape[1]
    nb = cmp_ref.shape[1]
    qpos = pl.program_id(1) * tq + lax.broadcasted_iota(I32, (tq, 1), 0)
    kpos = lax.broadcasted_iota(I32, (1, t), 1)
    causal = kpos <= qpos
    winf = jnp.where(causal & (kpos > qpos - WINDOW), 1.0, 0.0)
    blk = lax.broadcasted_iota(I32, (1, nb), 1)
    cmpf = jnp.where((blk + 1) * CMP_BLOCK - 1 <= qpos, 1.0, 0.0)
    cur = qpos // CMP_BLOCK
    expand = jnp.where(lax.broadcasted_iota(I32, (nb, t), 1) // CMP_BLOCK == lax.broadcasted_iota(I32, (nb, t), 0),
                       1.0, 0.0).astype(BF16)
    gates = jax.nn.sigmoid(gt_ref[...])
    scale = HEAD_DIM ** -0.5
    cmpkv = cmp_ref[0].astype(BF16)
    slc = slc_ref[0]
    win = win_ref[0]
    for g in range(KVH_C):
        ksl = slice(g * LANES, (g + 1) * LANES)
        vsl = slice((KVH_C + g) * LANES, (KVH_C + g + 1) * LANES)
        o_cmp = []
        imp = jnp.zeros((tq, nb), F32)
        qs = []
        for n in range(HPG_C):
            sl = slice((g * HPG_C + n) * LANES, (g * HPG_C + n + 1) * LANES)
            q = q_ref[:, sl].astype(BF16)
            qs.append(q)
            p, den = _softmax_parts(_nt(q, cmpkv[:, ksl]) * scale, cmpf)
            pc = p / jnp.maximum(den, 1e-30)
            imp = imp + pc
            o_cmp.append(jnp.dot(pc.astype(BF16), cmpkv[:, vsl], preferred_element_type=F32))
        sel = _top_blocks(_block_importance(imp, cur, nb), min(SEL_BLOCKS, nb))
        slcf = jnp.where(causal, jnp.dot(sel.astype(BF16), expand, preferred_element_type=F32), 0.0)
        for n in range(HPG_C):
            h = g * HPG_C + n
            sl = slice(h * LANES, (h + 1) * LANES)
            p, den = _softmax_parts(_nt(qs[n], slc[:, ksl]) * scale, slcf)
            o_slc = jnp.dot(p.astype(BF16), slc[:, vsl], preferred_element_type=F32) / jnp.maximum(den, 1e-30)
            p, den = _softmax_parts(_nt(qs[n], win[:, ksl]) * scale, winf)
            o_win = jnp.dot(p.astype(BF16), win[:, vsl], preferred_element_type=F32) / jnp.maximum(den, 1e-30)
            o = (o_cmp[n] * gates[:, 3 * h:3 * h + 1] + o_slc * gates[:, 3 * h + 1:3 * h + 2]
                 + o_win * gates[:, 3 * h + 2:3 * h + 3])
            o_ref[:, sl] = (o * _silu(z_ref[:, sl])).astype(o_ref.dtype)


def _nsa_prompt_call(u, post, cmpkv, b, t):
    tq = 128
    nq = t // tq
    nb = t // CMP_BLOCK

    def rows(width, blk=0):
        return pl.BlockSpec((tq, width), lambda bi, i: (bi * nq + i, blk))

    return pl.pallas_call(
        _nsa_prompt_kernel,
        grid=(b, nq),
        in_specs=[
            rows(768), rows(768, _OFF["c_z"] // 768), rows(128, _OFF["c_g"] // 128),
            pl.BlockSpec((1, nb, 512), lambda bi, i: (bi, 0, 0)),
            pl.BlockSpec((1, t, 512), lambda bi, i: (bi, 0, 0)),
            pl.BlockSpec((1, t, 512), lambda bi, i: (bi, 0, 0)),
        ],
        out_specs=rows(768),
        out_shape=jax.ShapeDtypeStruct((b * t, 768), BF16),
        compiler_params=_params(("parallel", "arbitrary")),
        name="nsa_prompt",
    )(post["qc"], u, u, cmpkv, post["slc_h"].reshape(b, t, 512), post["win_h"].reshape(b, t, 512))


def _page_specs(n_pages, width, layer):
    return [pl.BlockSpec((None, None, PAGE_SIZE, width), functools.partial(_page_index, layer=layer, page=p))
            for p in range(n_pages)]


def _page_index(b, pt_ref, *, layer, page):
    return (layer, pt_ref[b, page], 0, 0)


def _seq_spec(*shape):
    nd = len(shape)
    return pl.BlockSpec((1,) + shape, lambda b, pt: (b,) + (0,) * nd)


def _dsa_score_kernel(pt_ref, qi_ref, wi_ref, kin_ref, *rest):
    del pt_ref
    pages, o_ref = rest[:-1], rest[-1]
    qi = qi_ref[0]
    w = wi_ref[0] * (IDX_HEADS ** -0.5)
    qh = qi.astype(BF16)
    for p, page in enumerate(pages):
        s = _nt(qh, page[...].astype(BF16))
        rel = jnp.maximum(s * (IDX_DIM ** -0.5), 0.0) * w
        o_ref[0, :, p * PAGE_SIZE:(p + 1) * PAGE_SIZE] = jnp.sum(rel, axis=0, keepdims=True)
    s_new = jnp.sum(qi * kin_ref[0], axis=-1, keepdims=True)
    sc = jnp.sum(jnp.maximum(s_new * (IDX_DIM ** -0.5), 0.0) * w, axis=0, keepdims=True)
    lane = lax.broadcasted_iota(I32, (1, LANES), 1)
    n_past = len(pages) * PAGE_SIZE
    o_ref[0, :, n_past:n_past + LANES] = jnp.where(lane == 0, sc, -jnp.inf)


def _dsa_score_call(qi, wi, ki_new, cache_kidx, page_table, layer):
    bs, n_pages = page_table.shape
    lk = n_pages * PAGE_SIZE + LANES
    grid_spec = pltpu.PrefetchScalarGridSpec(
        num_scalar_prefetch=1, grid=(bs,),
        in_specs=[_seq_spec(IDX_HEADS, IDX_DIM), _seq_spec(IDX_HEADS, 1), _seq_spec(1, IDX_DIM)]
        + _page_specs(n_pages, IDX_DIM, layer),
        out_specs=_seq_spec(1, lk))
    return pl.pallas_call(
        _dsa_score_kernel, grid_spec=grid_spec,
        out_shape=jax.ShapeDtypeStruct((bs, 1, lk), F32),
        compiler_params=_params(("arbitrary",)),
        name="dsa_score",
    )(page_table, qi.reshape(bs, IDX_HEADS, IDX_DIM), wi.reshape(bs, IDX_HEADS, 1),
      ki_new.reshape(bs, 1, IDX_DIM), *([cache_kidx] * n_pages))


def _select_kernel(s_ref, o_ref, key_ref, *, n_valid, n_keys):
    r, l = s_ref.shape
    kpos = lax.broadcasted_iota(I32, (1, l), 1)
    _topk_mask(s_ref[...], jnp.broadcast_to(kpos < n_valid, (r, l)), n_keys, key_ref, o_ref)


def _select_call(scores, n_valid, n_keys):
    r, l = scores.shape
    return pl.pallas_call(
        functools.partial(_select_kernel, n_valid=n_valid, n_keys=n_keys),
        grid=(1,),
        in_specs=[pl.BlockSpec((r, l), lambda i: (0, 0))],
        out_specs=pl.BlockSpec((r, l), lambda i: (0, 0)),
        out_shape=jax.ShapeDtypeStruct((r, l), F32),
        scratch_shapes=[pltpu.VMEM((r, l), I32)],
        compiler_params=_params(("arbitrary",)),
        name="dsa_select",
    )(scores)


def _decode_attend(q, maskf, mask_new, pages, new_row, ksl, scale):
    qh = q.astype(BF16)
    s = jnp.concatenate([_nt(qh, pg[:, ksl].astype(BF16)) for pg in pages], axis=1) * scale
    s_new = jnp.sum(q * new_row[:, ksl], axis=-1, keepdims=True) * scale
    sm = jnp.where(maskf > 0.0, s, NEG)
    sn = jnp.where(mask_new > 0.0, s_new, NEG)
    m = jnp.maximum(jnp.max(sm, axis=-1, keepdims=True), sn)
    p = jnp.exp(sm - m) * maskf
    p_new = jnp.exp(sn - m) * mask_new
    den = jnp.sum(p, axis=-1, keepdims=True) + p_new
    return p, p_new, den


def _decode_pv(p, p_new, pages, new_row, vsl):
    ph = p.astype(BF16)
    o = p_new * new_row[:, vsl]
    for i, pg in enumerate(pages):
        rows = pg.shape[0]
        off = sum(x.shape[0] for x in pages[:i])
        o = o + jnp.dot(ph[:, off:off + rows], pg[:, vsl].astype(BF16), preferred_element_type=F32)
    return o


def _dsa_attn_kernel(pt_ref, q_ref, z_ref, m_ref, new_ref, *rest):
    del pt_ref
    pages, o_ref = rest[:-1], rest[-1]
    n_past = len(pages) * PAGE_SIZE
    mrow = m_ref[0]
    maskf = mrow[:, :n_past]
    mask_new = mrow[:, n_past:n_past + 1]
    new_row = new_ref[0]
    for g in range(KVH_A):
        ksl = slice(g * LANES, (g + 1) * LANES)
        vsl = slice((KVH_A + g) * LANES, (KVH_A + g + 1) * LANES)
        hs = slice(g * HPG_A, (g + 1) * HPG_A)
        qg = q_ref[0, hs, :]
        p, p_new, den = _decode_attend(qg, maskf, mask_new, pages, new_row, ksl, HEAD_DIM ** -0.5)
        o = _decode_pv(p, p_new, pages, new_row, vsl) / jnp.maximum(den, 1e-30)
        o_ref[0, hs, :] = (o * _silu(z_ref[0, hs, :])).astype(o_ref.dtype)


def _dsa_attn_call(q, z, maskf, kv_new, cache_kv, page_table, layer):
    bs, n_pages = page_table.shape
    lk = maskf.shape[-1]
    grid_spec = pltpu.PrefetchScalarGridSpec(
        num_scalar_prefetch=1, grid=(bs,),
        in_specs=[_seq_spec(H_A, HEAD_DIM), _seq_spec(H_A, HEAD_DIM), _seq_spec(1, lk), _seq_spec(1, 512)]
        + _page_specs(n_pages, 512, layer),
        out_specs=_seq_spec(H_A, HEAD_DIM))
    return pl.pallas_call(
        _dsa_attn_kernel, grid_spec=grid_spec,
        out_shape=jax.ShapeDtypeStruct((bs, H_A, HEAD_DIM), BF16),
        compiler_params=_params(("arbitrary",)),
        name="dsa_attn",
    )(page_table, q.reshape(bs, H_A, HEAD_DIM), z.reshape(bs, H_A, HEAD_DIM), maskf.reshape(bs, 1, lk),
      kv_new.reshape(bs, 1, 512), *([cache_kv] * n_pages)).reshape(bs, H_A * HEAD_DIM)


def _diff_dec_kernel(pt_ref, q_ref, z_ref, new_ref, lam_ref, g_ref, *rest, layer):
    del pt_ref
    pages, o_ref = rest[:-1], rest[-1]
    n_past = len(pages) * PAGE_SIZE
    new_row = new_ref[0]
    lam, lam_init = _lam_value(lam_ref, layer)
    lane = lax.broadcasted_iota(I32, (2, LANES), 1)
    rowi = lax.broadcasted_iota(I32, (2, LANES), 0)
    halfsel = (rowi == 0) == (lane < HALF_DIM)
    ones = jnp.ones((1, n_past), F32)
    one = jnp.ones((1, 1), F32)
    for h in range(H_B):
        ksl = slice(h * LANES, (h + 1) * LANES)
        vsl = slice((H_B + h) * LANES, (H_B + h + 1) * LANES)
        qm = jnp.where(halfsel, jnp.broadcast_to(q_ref[0, h:h + 1, :], (2, LANES)), 0.0)
        p, p_new, den = _decode_attend(qm, ones, one, pages, new_row, ksl, HALF_DIM ** -0.5)
        inv = 1.0 / jnp.maximum(den, 1e-30)
        p, p_new = p * inv, p_new * inv
        a = p[0:1] - lam * p[1:2]
        a_new = p_new[0:1] - lam * p_new[1:2]
        o = _subln(_decode_pv(a, a_new, pages, new_row, vsl), g_ref[...], lam_init)
        o_ref[0, h:h + 1, :] = (o * _silu(z_ref[0, h:h + 1, :])).astype(o_ref.dtype)


def _diff_dec_call(q, z, kv_new, cache_kv, page_table, lam_l, subln_l, layer):
    bs, n_pages = page_table.shape
    grid_spec = pltpu.PrefetchScalarGridSpec(
        num_scalar_prefetch=1, grid=(bs,),
        in_specs=[_seq_spec(H_B, HEAD_DIM), _seq_spec(H_B, HEAD_DIM), _seq_spec(1, 1024),
                  pl.BlockSpec((4, HALF_DIM), lambda b, pt: (0, 0)),
                  pl.BlockSpec((1, HEAD_DIM), lambda b, pt: (0, 0))]
        + _page_specs(n_pages, 1024, layer),
        out_specs=_seq_spec(H_B, HEAD_DIM))
    return pl.pallas_call(
        functools.partial(_diff_dec_kernel, layer=layer), grid_spec=grid_spec,
        out_shape=jax.ShapeDtypeStruct((bs, H_B, HEAD_DIM), BF16),
        compiler_params=_params(("arbitrary",)),
        name="diff_dec",
    )(page_table, q.reshape(bs, H_B, HEAD_DIM), z.reshape(bs, H_B, HEAD_DIM), kv_new.reshape(bs, 1, 1024),
      lam_l, subln_l.reshape(1, HEAD_DIM), *([cache_kv] * n_pages)).reshape(bs, H_B * HEAD_DIM)


def _nsa_dec_kernel(pt_ref, q_ref, z_ref, gt_ref, slcn_ref, winn_ref, a_ref, pe_ref, win_ref, *rest):
    del pt_ref
    o_ref, cmp_scr = rest[-2], rest[-1]
    n_pages = (len(rest) - 2) // 2
    cmp_pages, slc_pages = rest[:n_pages], rest[n_pages:2 * n_pages]
    n_past = n_pages * PAGE_SIZE
    nb = n_past // CMP_BLOCK
    per_page = PAGE_SIZE // CMP_BLOCK
    scale = HEAD_DIM ** -0.5
    gates = jax.nn.sigmoid(gt_ref[0])
    a = a_ref[...]
    pe_term = jnp.sum(pe_ref[...] * a, axis=0, keepdims=True)
    cmp_scr[nb:, :] = jnp.zeros((LANES - nb, 512), F32)
    for p, page in enumerate(cmp_pages):
        for j in range(per_page):
            bj = p * per_page + j
            blk = jnp.sum(page[j * CMP_BLOCK:(j + 1) * CMP_BLOCK, :] * a, axis=0, keepdims=True) + pe_term
            cmp_scr[bj:bj + 1, :] = blk
    cmph = cmp_scr[...].astype(BF16)
    cur = n_past // CMP_BLOCK
    nblk = cur + 1
    lane = lax.broadcasted_iota(I32, (1, LANES), 1)
    cmpf = jnp.where(lane < nb, 1.0, 0.0)
    win = win_ref[...]
    wpos = lax.broadcasted_iota(I32, (1, win.shape[0]), 1)
    winf = jnp.where(wpos >= win.shape[0] + 1 - WINDOW, 1.0, 0.0)
    one = jnp.ones((1, 1), F32)
    slc_new = slcn_ref[0]
    win_new = winn_ref[0]
    for g in range(KVH_C):
        ksl = slice(g * LANES, (g + 1) * LANES)
        vsl = slice((KVH_C + g) * LANES, (KVH_C + g + 1) * LANES)
        qg = q_ref[0, g * HPG_C:(g + 1) * HPG_C, :]
        p, den = _softmax_parts(_nt(qg.astype(BF16), cmph[:, ksl]) * scale, cmpf)
        pc = p / jnp.maximum(den, 1e-30)
        o_cmp = jnp.dot(pc.astype(BF16), cmph[:, vsl], preferred_element_type=F32)
        imp = jnp.sum(pc, axis=0, keepdims=True)
        sel = _top_blocks(_block_importance(imp, cur, nblk), min(SEL_BLOCKS, nblk))
        tok = []
        for p in range(n_pages):
            m = jnp.zeros((1, PAGE_SIZE), F32)
            for j in range(per_page):
                bj = p * per_page + j
                m = jnp.where((lane // CMP_BLOCK) == j, sel[:, bj:bj + 1], m)
            tok.append(m)
        slcf = jnp.concatenate(tok, axis=1)
        sel_new = sel[:, cur:cur + 1]
        p, p_new, den = _decode_attend(qg, slcf, sel_new, slc_pages, slc_new, ksl, scale)
        o_slc = _decode_pv(p, p_new, slc_pages, slc_new, vsl) / jnp.maximum(den, 1e-30)
        p, p_new, den = _decode_attend(qg, winf, one, [win], win_new, ksl, scale)
        o_win = _decode_pv(p, p_new, [win], win_new, vsl) / jnp.maximum(den, 1e-30)
        for n in range(HPG_C):
            h = g * HPG_C + n
            o = (o_cmp[n:n + 1] * gates[:, 3 * h:3 * h + 1] + o_slc[n:n + 1] * gates[:, 3 * h + 1:3 * h + 2]
                 + o_win[n:n + 1] * gates[:, 3 * h + 2:3 * h + 3])
            o_ref[0, h:h + 1, :] = (o * _silu(z_ref[0, h:h + 1, :])).astype(o_ref.dtype)


def _nsa_dec_call(q, z, gt, slc_new, win_new, alpha_e, pe_e, win_state, cache_cmp, cache_slc, page_table, layer):
    bs, n_pages = page_table.shape
    wb = win_state.shape[2]
    grid_spec = pltpu.PrefetchScalarGridSpec(
        num_scalar_prefetch=1, grid=(bs,),
        in_specs=[_seq_spec(H_C, HEAD_DIM), _seq_spec(H_C, HEAD_DIM), _seq_spec(1, LANES),
                  _seq_spec(1, 512), _seq_spec(1, 512),
                  pl.BlockSpec((CMP_BLOCK, 512), lambda b, pt: (0, 0)),
                  pl.BlockSpec((CMP_BLOCK, 512), lambda b, pt: (0, 0)),
                  pl.BlockSpec((None, None, wb, 512), lambda b, pt: (layer, b, 0, 0))]
        + _page_specs(n_pages, 512, layer) + _page_specs(n_pages, 512, layer),
        out_specs=_seq_spec(H_C, HEAD_DIM),
        scratch_shapes=[pltpu.VMEM((LANES, 512), F32)])
    return pl.pallas_call(
        _nsa_dec_kernel, grid_spec=grid_spec,
        out_shape=jax.ShapeDtypeStruct((bs, H_C, HEAD_DIM), BF16),
        compiler_params=_params(("arbitrary",)),
        name="nsa_dec",
    )(page_table, q.reshape(bs, H_C, HEAD_DIM), z.reshape(bs, H_C, HEAD_DIM), gt.reshape(bs, 1, LANES),
      slc_new.reshape(bs, 1, 512), win_new.reshape(bs, 1, 512), alpha_e, pe_e, win_state,
      *([cache_cmp] * n_pages), *([cache_slc] * n_pages)).reshape(bs, H_C * HEAD_DIM)


def _outproj_kernel(ma_ref, mb_ref, mc_ref, w_ref, x_ref, gate_ref, fg_ref, o_ref, *, final):
    wa, wb = H_A * HEAD_DIM, (H_A + H_B) * HEAD_DIM
    out = jnp.dot(ma_ref[...], w_ref[0:wa, :], preferred_element_type=F32)
    out = out + jnp.dot(mb_ref[...], w_ref[wa:wb, :], preferred_element_type=F32)
    out = out + jnp.dot(mc_ref[...], w_ref[wb:, :], preferred_element_type=F32)
    xn = x_ref[...] + gate_ref[...] * out
    if final:
        xn = xn * lax.rsqrt(jnp.mean(xn * xn, axis=-1, keepdims=True) + NORM_EPS) * fg_ref[...]
    o_ref[...] = xn


def _outproj_call(ma, mb, mc, w, x, gate, fg, rows_per_batch, final):
    m, d = x.shape
    tm = min(m, 256)
    if rows_per_batch > 1:
        per = rows_per_batch // tm
        gspec = pl.BlockSpec((None, 1, d), lambda i: (i // per, 0, 0))
    else:
        gspec = pl.BlockSpec((tm, d), lambda i: (i, 0))

    def rows(width):
        return pl.BlockSpec((tm, width), lambda i: (i, 0))

    return pl.pallas_call(
        functools.partial(_outproj_kernel, final=final),
        grid=(m // tm,),
        in_specs=[rows(768), rows(512), rows(768), pl.BlockSpec((d, d), lambda i: (0, 0)), rows(d), gspec,
                  pl.BlockSpec((1, d), lambda i: (0, 0))],
        out_specs=rows(d),
        out_shape=jax.ShapeDtypeStruct((m, d), F32),
        compiler_params=_params(("parallel",)),
        name="outproj",
    )(ma, mb, mc, w, x, gate, fg)


def _pad_w_in(w_in):
    src, o = {}, 0
    for name, width in _IN_SPLITS:
        src[name] = (o, width)
        o += width
    parts = []
    for name, width in _PAD_ORDER:
        if name in src:
            s, sw = src[name]
            parts.append(w_in[..., s:s + sw])
            if width > sw:
                parts.append(jnp.zeros(w_in.shape[:-1] + (width - sw,), w_in.dtype))
        else:
            parts.append(jnp.zeros(w_in.shape[:-1] + (width,), w_in.dtype))
    return jnp.concatenate(parts, axis=-1).astype(BF16)


def _ucol(u, name, width=None):
    width = width or dict(_PAD_ORDER)[name]
    return u[:, _OFF[name]:_OFF[name] + width]


def kernel(x_prompt, x_sample, cache_a_kv, cache_a_kidx, cache_b_kv, cache_c_cmp_kv, cache_c_slc_kv,
           state_c_win_kv, page_table, c_prompt, c_sample, w_mod, b_mod, norm_g, w_in, w_out, lam,
           subln_g, cmp_alpha, cmp_pe, final_g):
    depth, d, _ = w_mod.shape
    b, t, _ = x_prompt.shape
    bs, ts, _ = x_sample.shape
    assert ts == 1 and t % 256 == 0
    n_pool = cache_a_kv.shape[1]
    n_pages = page_table.shape[1]
    n_past = n_pages * PAGE_SIZE
    wbuf = state_c_win_kv.shape[2]
    assert wbuf == WINDOW and n_past >= WINDOW

    w_in_p = _pad_w_in(w_in)
    w_out_h = w_out.astype(BF16)
    mod = _mod_call(jnp.concatenate([c_prompt, c_sample], axis=0), w_mod, b_mod)
    alpha_e = jnp.broadcast_to(cmp_alpha[..., None], cmp_alpha.shape + (HEAD_DIM,)).reshape(depth, CMP_BLOCK, 512)
    pe_e = cmp_pe.reshape(depth, CMP_BLOCK, 512)
    fg = final_g.reshape(1, d)

    ca_kv = cache_a_kv.reshape(depth, n_pool, PAGE_SIZE, 512)
    cb_kv = cache_b_kv.reshape(depth, n_pool, PAGE_SIZE, 1024)
    cc_cmp = cache_c_cmp_kv.reshape(depth, n_pool, PAGE_SIZE, 512)
    cc_slc = cache_c_slc_kv.reshape(depth, n_pool, PAGE_SIZE, 512)
    win_state = state_c_win_kv.reshape(depth, bs, wbuf, 512)

    tabs_p = _rope_tables(jnp.arange(t, dtype=I32))
    tabs_s = _rope_tables(n_past + jnp.arange(1, dtype=I32))

    x = x_prompt.reshape(b * t, d)
    st_p = []
    for layer in range(depth):
        shift, scale, gate = (mod[layer, :b, i * d:(i + 1) * d].reshape(b, 1, d) for i in range(3))
        u = _inproj_call(x, norm_g[layer].reshape(1, d), scale, shift, w_in_p[layer], t)
        post = _post_call(u, tabs_p, t)
        ya = _dsa_prompt_call(u, post, b, t)
        yb = _diff_prompt_call(u, post, lam[layer], subln_g[layer], layer, b, t)
        cmpkv = _compress_call(post["cmp"], alpha_e[layer], pe_e[layer], b, t)
        yc = _nsa_prompt_call(u, post, cmpkv, b, t)
        x = _outproj_call(ya, yb, yc, w_out_h[layer], x, gate, fg, t, layer == depth - 1)
        st_p.append(post)
    y_prompt = x.reshape(b, t, d)

    x = x_sample.reshape(bs, d)
    st_s = []
    for layer in range(depth):
        shift, scale, gate = (mod[layer, b:, i * d:(i + 1) * d] for i in range(3))
        u = _inproj_call(x, norm_g[layer].reshape(1, d), scale, shift, w_in_p[layer], 1)
        post = _post_call(u, tabs_s, 1)
        n_keys = min(TOPK_MAX, (n_past + 1) // 4)
        scores = _dsa_score_call(post["qi"], _ucol(u, "a_wi", IDX_HEADS), post["a_kidx"], cache_a_kidx,
                                 page_table, layer)
        maskf = _select_call(scores.reshape(bs, -1), n_past + 1, n_keys)
        ya = _dsa_attn_call(post["qa"], _ucol(u, "a_z"), maskf, post["a_kv"], ca_kv, page_table, layer)
        yb = _diff_dec_call(post["qb"], _ucol(u, "b_z"), post["b_kv"], cb_kv, page_table, lam[layer],
                            subln_g[layer], layer)
        yc = _nsa_dec_call(post["qc"], _ucol(u, "c_z"), _ucol(u, "c_g"), post["slc"], post["win"],
                           alpha_e[layer], pe_e[layer], win_state, cc_cmp, cc_slc, page_table, layer)
        x = _outproj_call(ya, yb, yc, w_out_h[layer], x, gate, fg, 1, layer == depth - 1)
        st_s.append(post)
    y_sample = x.reshape(bs, 1, d)

    def stack(states, name, bb, tt, tail):
        return jnp.stack([s[name] for s in states]).reshape((depth, bb, tt) + tail)

    outs = []
    for name, tail in (("a_kv", (2, KVH_A, HEAD_DIM)), ("a_kidx", (IDX_DIM,)), ("b_kv", (2, H_B, HEAD_DIM)),
                       ("cmp", (2, KVH_C, HEAD_DIM)), ("slc", (2, KVH_C, HEAD_DIM))):
        outs.append(stack(st_p, name, b, t, tail))
        outs.append(stack(st_s, name, bs, 1, tail))
    keep = min(WINDOW, t)
    win_p = stack(st_p, "win", b, t, (2, KVH_C, HEAD_DIM))[:, :, t - keep:]
    win_new = stack(st_s, "win", bs, 1, (2, KVH_C, HEAD_DIM))
    win_s = jnp.concatenate([state_c_win_kv[:, :, 1:], win_new], axis=2)
    return (y_prompt, y_sample, *outs, win_p, win_s)
```

```python
import functools
import math

import jax
import jax.numpy as jnp
from jax import lax
from jax.experimental import pallas as pl
from jax.experimental.pallas import tpu as pltpu

F32 = jnp.float32
BF16 = jnp.bfloat16
I32 = jnp.int32

HEAD_DIM = 128
H_A, H_B, H_C = 6, 4, 6
KVH_A, KVH_C = 2, 2
HPG_A, HPG_C = H_A // KVH_A, H_C // KVH_C
IDX_HEADS, IDX_DIM = 16, 64
TOPK_MAX = 256
HALF_DIM = 64
SUBLN_EPS = 1e-5
CMP_BLOCK = 64
SEL_BLOCKS = 8
WINDOW = 512
ROPE_THETA = 10000.0
NORM_EPS = 1e-6
PAGE_SIZE = 128

LANES = 128
VMEM_LIMIT = 56 * 1024 * 1024

NEG = -1e30
INT_MIN = -2147483648
SLABS_A = 2 * KVH_A
SLABS_B = 2 * H_B
SLABS_C = 2 * KVH_C
PROMPT_BANDS = 4

_IN_SPLITS = (
    ("a_q", 768), ("a_k", 256), ("a_v", 256), ("a_qi", 1024), ("a_ki", 64), ("a_wi", 16), ("a_z", 768),
    ("b_q", 512), ("b_k", 512), ("b_v", 512), ("b_z", 512),
    ("c_q", 768), ("c_kc", 256), ("c_vc", 256), ("c_ks", 256), ("c_vs", 256), ("c_kw", 256),
    ("c_vw", 256), ("c_g", 18), ("c_z", 768),
)
_PAD_ORDER = (
    ("a_q", 768), ("a_z", 768), ("c_q", 768), ("c_z", 768), ("a_qi", 1024),
    ("b_k", 512), ("b_v", 512), ("b_q", 512), ("b_z", 512),
    ("a_k", 256), ("a_v", 256), ("c_kc", 256), ("c_vc", 256), ("c_ks", 256), ("c_vs", 256),
    ("c_kw", 256), ("c_vw", 256), ("a_ki", 128), ("a_wi", 128), ("c_g", 128), ("_pad", 128),
)
_OFF = {}
_o = 0
for _n, _w in _PAD_ORDER:
    _OFF[_n] = _o
    _o += _w
NP = _o
IN_TN = 512


def _params(sem):
    return pltpu.CompilerParams(dimension_semantics=sem, vmem_limit_bytes=VMEM_LIMIT)


def _nt(a, b):
    return lax.dot_general(a, b, (((1,), (1,)), ((), ())), preferred_element_type=F32)


def _silu(z):
    return z * jax.nn.sigmoid(z)


def _slab(ref, c, n_slabs):
    return ref[pl.ds(c, ref.shape[0] // n_slabs, stride=n_slabs), :]


def _mod_kernel(c_ref, w_ref, b_ref, o_ref):
    c = c_ref[...]
    o_ref[0] = jnp.dot(_silu(c), w_ref[0], preferred_element_type=F32,
                       precision=lax.Precision.HIGHEST) + b_ref[0]


def _mod_call(c_all, w_mod, b_mod):
    depth, d, n3 = w_mod.shape
    bc = c_all.shape[0]
    tn = 768
    return pl.pallas_call(
        _mod_kernel,
        grid=(depth, n3 // tn),
        in_specs=[
            pl.BlockSpec((bc, d), lambda l, j: (0, 0)),
            pl.BlockSpec((1, d, tn), lambda l, j: (l, 0, j)),
            pl.BlockSpec((1, 1, tn), lambda l, j: (l, 0, j)),
        ],
        out_specs=pl.BlockSpec((1, bc, tn), lambda l, j: (l, 0, j)),
        out_shape=jax.ShapeDtypeStruct((depth, bc, n3), F32),
        compiler_params=_params(("parallel", "parallel")),
        name="mod",
    )(c_all, w_mod, b_mod.reshape(depth, 1, n3))


def _inproj_kernel(x_ref, g_ref, sc_ref, sh_ref, w_ref, o_ref, h_ref):
    @pl.when(pl.program_id(1) == 0)
    def _():
        x = x_ref[...]
        y = x * lax.rsqrt(jnp.mean(x * x, axis=-1, keepdims=True) + NORM_EPS) * g_ref[...]
        h_ref[...] = (y * (1.0 + sc_ref[...]) + sh_ref[...]).astype(BF16)

    o_ref[...] = jnp.dot(h_ref[...], w_ref[...], preferred_element_type=F32)


def _inproj_call(x, g, scale, shift, w, rows_per_batch):
    m, d = x.shape
    tm = min(m, 1024)
    if rows_per_batch > 1:
        per = rows_per_batch // tm
        mspec = pl.BlockSpec((None, 1, d), lambda i, j: (i // per, 0, 0))
    else:
        mspec = pl.BlockSpec((tm, d), lambda i, j: (i, 0))
    return pl.pallas_call(
        _inproj_kernel,
        grid=(m // tm, NP // IN_TN),
        in_specs=[
            pl.BlockSpec((tm, d), lambda i, j: (i, 0)),
            pl.BlockSpec((1, d), lambda i, j: (0, 0)),
            mspec, mspec,
            pl.BlockSpec((d, IN_TN), lambda i, j: (0, j)),
        ],
        out_specs=pl.BlockSpec((tm, IN_TN), lambda i, j: (i, j)),
        out_shape=jax.ShapeDtypeStruct((m, NP), F32),
        scratch_shapes=[pltpu.VMEM((tm, d), BF16)],
        compiler_params=_params(("parallel", "arbitrary")),
        name="inproj",
    )(x, g, scale, shift, w)


def _rope128(x, cos, sin_s):
    return x * cos + pltpu.roll(x, 64, 1) * sin_s


def _rope64(x, cos, sin_s, low32):
    partner = jnp.where(low32, pltpu.roll(x, 96, 1), pltpu.roll(x, 32, 1))
    return x * cos + partner * sin_s


_POST_STATE = ("a_kv", "b_kv", "cmp", "slc", "win", "a_kidx")


def _post_kernel(*refs, n_alias):
    (c128_ref, s128_ref, c64_ref, s64_ref,
     uqa_ref, uqc_ref, uqi_ref, ubkv_ref, ubq_ref, uakv_ref, ucmp_ref, uslc_ref, uwin_ref, uki_ref) = refs[:14]
    (qa_ref, qc_ref, qi_ref, qb_ref, akv_ref, bkv_ref, cmp_ref, slc_ref, win_ref, ki_ref,
     bkvh_ref, akvh_ref, slch_ref, winh_ref, kid_ref) = refs[14 + n_alias:]
    c128, s128 = c128_ref[...], s128_ref[...]
    c64, s64 = c64_ref[...], s64_ref[...]
    lane = lax.broadcasted_iota(I32, (1, LANES), 1)
    low32 = (lane % 64) < 32
    tq = uqa_ref.shape[0]

    def rot128(x):
        return _rope128(x, c128, s128)

    def rot64(x):
        return _rope64(x, c64, s64, low32)

    def queries(src, dst, n, rot):
        for h in range(n):
            sl = slice(h * LANES, (h + 1) * LANES)
            dst[:, sl] = rot(src[:, sl])

    def cache_rows(src, dst, dsth, n_slabs, rot):
        for c in range(n_slabs):
            sl = slice(c * LANES, (c + 1) * LANES)
            val = rot(src[:, sl]) if c < n_slabs // 2 else src[:, sl]
            dst[pl.ds(c, tq, stride=n_slabs), :] = val
            if dsth is not None:
                dsth[:, sl] = val.astype(BF16)

    queries(uqa_ref, qa_ref, H_A, rot128)
    queries(uqc_ref, qc_ref, H_C, rot128)
    queries(uqi_ref, qi_ref, IDX_HEADS // 2, rot64)
    queries(ubq_ref, qb_ref, H_B, rot64)
    cache_rows(ubkv_ref, bkv_ref, bkvh_ref, SLABS_B, rot64)
    cache_rows(uakv_ref, akv_ref, akvh_ref, SLABS_A, rot128)
    cache_rows(ucmp_ref, cmp_ref, None, SLABS_C, rot128)
    cache_rows(uslc_ref, slc_ref, slch_ref, SLABS_C, rot128)
    cache_rows(uwin_ref, win_ref, winh_ref, SLABS_C, rot128)
    ki = rot64(uki_ref[...])
    ki_ref[...] = ki[:, :IDX_DIM]
    kid_ref[...] = (ki + pltpu.roll(ki, 64, 1)).astype(BF16)


def _post_call(u, tabs, rows_per_batch, layer, depth, prev):
    m = u.shape[0]
    tq = min(m, 256)
    if rows_per_batch > 1:
        per = rows_per_batch // tq
        tspec = pl.BlockSpec((tq, LANES), lambda i: (i % per, 0))
    else:
        tspec = pl.BlockSpec((1, LANES), lambda i: (0, 0))

    def col(name, width):
        blk = _OFF[name] // width
        assert blk * width == _OFF[name]
        return pl.BlockSpec((tq, width), lambda i: (i, blk))

    def row(width):
        return pl.BlockSpec((tq, width), lambda i: (i, 0))

    def state(rows_per_token, width=LANES):
        return pl.BlockSpec((None, tq * rows_per_token, width), lambda i: (layer, i, 0))

    in_specs = [tspec] * 4 + [
        col("a_q", 768), col("c_q", 768), col("a_qi", 1024), col("b_k", 1024), col("b_q", 512),
        col("a_k", 512), col("c_kc", 512), col("c_ks", 512), col("c_kw", 512), col("a_ki", 128),
    ]
    args = list(tabs) + [u] * 10
    aliases = {}
    if prev is not None:
        for k, name in enumerate(_POST_STATE):
            aliases[len(args)] = 4 + k
            args.append(prev[name])
            in_specs.append(pl.BlockSpec(memory_space=pl.ANY))
    qouts = [(768, F32), (768, F32), (1024, F32), (512, F32)]
    houts = [(1024, BF16), (512, BF16), (512, BF16), (512, BF16), (128, BF16)]
    souts = [(SLABS_A, LANES), (SLABS_B, LANES), (SLABS_C, LANES), (SLABS_C, LANES), (SLABS_C, LANES), (1, IDX_DIM)]
    res = pl.pallas_call(
        functools.partial(_post_kernel, n_alias=len(aliases)),
        grid=(m // tq,),
        in_specs=in_specs,
        out_specs=[row(w) for w, _ in qouts] + [state(r, w) for r, w in souts] + [row(w) for w, _ in houts],
        out_shape=[jax.ShapeDtypeStruct((m, w), dt) for w, dt in qouts]
        + [jax.ShapeDtypeStruct((depth, m * r, w), F32) for r, w in souts]
        + [jax.ShapeDtypeStruct((m, w), dt) for w, dt in houts],
        input_output_aliases=aliases,
        compiler_params=_params(("parallel",)),
        name="post",
    )(*args)
    names = ("qa", "qc", "qi", "qb") + _POST_STATE + ("b_kv_h", "a_kv_h", "slc_h", "win_h", "kidx_dup")
    return dict(zip(names, res))


def _rope_tables(pos):
    def tab(d):
        inv = ROPE_THETA ** (-jnp.arange(0, d, 2, dtype=F32) / d)
        ang = pos.astype(F32)[:, None] * inv[None, :]
        cos, sin = jnp.cos(ang), jnp.sin(ang)
        reps = LANES // d
        return jnp.tile(jnp.concatenate([cos, cos], -1), (1, reps)), jnp.tile(jnp.concatenate([-sin, sin], -1), (1, reps))

    c128, s128 = tab(HEAD_DIM)
    c64, s64 = tab(IDX_DIM)
    return c128, s128, c64, s64


def _softmax_parts(s, maskf):
    sm = jnp.where(maskf > 0.0, s, NEG)
    m = jnp.max(sm, axis=-1, keepdims=True)
    p = jnp.exp(sm - m) * maskf
    return p, jnp.sum(p, axis=-1, keepdims=True)


def _row_to_col(x):
    return jnp.broadcast_to(x, (LANES, x.shape[1])).T[:, 0:1]


def _topk_mask(score, valid, k, key_ref, keyt_ref, mask_ref):
    r, l = score.shape
    nchunk = l // LANES
    bits = lax.bitcast_convert_type(score + 0.0, I32)
    key = bits ^ ((bits >> 31) & 0x7FFFFFFF)
    key_ref[...] = jnp.where(valid, key, INT_MIN)
    for c in range(nchunk):
        sl = slice(c * LANES, (c + 1) * LANES)
        keyt_ref[sl, :] = lax.bitcast_convert_type(lax.bitcast_convert_type(key_ref[:, sl], F32).T, I32)
    kf = float(k)

    def count(cand, strict):
        acc = jnp.zeros((LANES, r), F32)
        for c in range(nchunk):
            kc = keyt_ref[c * LANES:(c + 1) * LANES, :]
            acc = acc + jnp.where((kc > cand) if strict else (kc >= cand), 1.0, 0.0)
        return jnp.sum(acc, axis=0, keepdims=True)

    def body(it, t_u):
        cand_u = t_u | lax.shift_left(jnp.int32(1), 31 - it)
        return jnp.where(count(cand_u ^ INT_MIN, False) >= kf, cand_u, t_u)

    t_u = lax.fori_loop(0, 32, body, jnp.zeros((1, r), I32))
    thr_t = t_u ^ INT_MIN
    need_t = kf - count(thr_t, True)
    tied_t = jnp.where(count(thr_t, False) > kf, jnp.where(thr_t != INT_MIN, 1.0, 0.0), 0.0)
    thr = lax.bitcast_convert_type(_row_to_col(lax.bitcast_convert_type(thr_t, F32)), I32)
    has_tie = jnp.max(tied_t) > 0.0

    @pl.when(jnp.logical_not(has_tie))
    def _():
        for c in range(nchunk):
            sl = slice(c * LANES, (c + 1) * LANES)
            kc = key_ref[:, sl]
            mask_ref[:, sl] = jnp.where(kc >= thr, jnp.where(kc != INT_MIN, 1.0, 0.0), 0.0)

    @pl.when(has_tie)
    def _():
        need = _row_to_col(need_t)
        ri = lax.broadcasted_iota(I32, (LANES, LANES), 0)
        ci = lax.broadcasted_iota(I32, (LANES, LANES), 1)
        tri = jnp.where(ri < ci, 1.0, 0.0).astype(BF16)
        carry = jnp.zeros((r, 1), F32)
        for c in range(nchunk):
            sl = slice(c * LANES, (c + 1) * LANES)
            kc = key_ref[:, sl]
            eq = jnp.where(kc == thr, 1.0, 0.0)
            pre = jnp.dot(eq.astype(BF16), tri, preferred_element_type=F32) + carry
            take = jnp.where(kc > thr, 1.0, jnp.where(pre < need, eq, 0.0))
            mask_ref[:, sl] = jnp.where(kc != INT_MIN, take, 0.0)
            carry = carry + jnp.sum(eq, axis=-1, keepdims=True)


def _top_blocks(imp, nsel, nblk):
    lane = lax.broadcasted_iota(I32, (1, imp.shape[1]), 1)
    rank = jnp.zeros(imp.shape, F32)
    for i in range(nblk):
        col = imp[:, i:i + 1]
        later = jnp.where(lane > i, 1.0, 0.0)
        rank = rank + jnp.where(col > imp, 1.0, jnp.where(col == imp, later, 0.0))
    return jnp.where(rank < float(nsel), jnp.where(imp > -jnp.inf, 1.0, 0.0), 0.0)


def _block_importance(imp, cur, nblk):
    blk = lax.broadcasted_iota(I32, imp.shape, 1)
    forced = (blk == 0) | (blk == cur) | (blk == cur - 1)
    imp = jnp.where(forced, jnp.inf, imp)
    return jnp.where((blk <= cur) & (blk < nblk), imp, -jnp.inf)


def _lam_value(lam_ref, layer):
    lp = lam_ref[...]
    a = jnp.sum(lp[0:1] * lp[1:2], axis=-1, keepdims=True)
    b = jnp.sum(lp[2:3] * lp[3:4], axis=-1, keepdims=True)
    lam_init = 0.8 - 0.6 * math.exp(-0.3 * layer)
    return jnp.exp(a) - jnp.exp(b) + lam_init, lam_init


def _subln(o, g, lam_init):
    y = o * lax.rsqrt(jnp.mean(o * o, axis=-1, keepdims=True) + SUBLN_EPS)
    return y * g * (1.0 - lam_init)


PROMPT_TQ = 128


def _bands(t):
    nq = t // PROMPT_TQ
    nb = PROMPT_BANDS if nq % PROMPT_BANDS == 0 else 1
    per = nq // nb
    return [(i * per, per, (i + 1) * per * PROMPT_TQ) for i in range(nb)]


def _band_rows(b, t, q0, width, blk=0):
    nq = t // PROMPT_TQ
    return pl.BlockSpec((PROMPT_TQ, width), lambda bi, i: (bi * nq + q0 + i, blk))


def _band_out(b, nqb, width):
    return (pl.BlockSpec((None, PROMPT_TQ, width), lambda bi, i: (bi, i, 0)),
            jax.ShapeDtypeStruct((b, nqb * PROMPT_TQ, width), BF16))


def _dsa_prompt_kernel(q_ref, qi_ref, wi_ref, z_ref, kid_ref, kv_ref, o_ref, key_ref, keyt_ref, mask_ref, *,
                       n_keys, q0):
    tq = q_ref.shape[0]
    t = kv_ref.shape[1]
    qpos = (q0 + pl.program_id(1)) * tq + lax.broadcasted_iota(I32, (tq, 1), 0)
    kpos = lax.broadcasted_iota(I32, (1, t), 1)
    lane = lax.broadcasted_iota(I32, (1, LANES), 1)
    lo = lane < IDX_DIM
    kd = kid_ref[0]
    w = wi_ref[...] * (IDX_HEADS ** -0.5 * IDX_DIM ** -0.5)
    acc = jnp.zeros((tq, t), F32)
    for pr in range(IDX_HEADS // 2):
        qp = qi_ref[:, pr * LANES:(pr + 1) * LANES]
        for half in range(2):
            qm = jnp.where(lo if half == 0 else jnp.logical_not(lo), qp, 0.0).astype(BF16)
            h = 2 * pr + half
            acc = acc + w[:, h:h + 1] * jnp.maximum(_nt(qm, kd), 0.0)
    _topk_mask(acc, kpos <= qpos, n_keys, key_ref, keyt_ref, mask_ref)
    maskf3 = _rep3(mask_ref[...])
    for g in range(KVH_A):
        q3 = jnp.concatenate([q_ref[:, (g * HPG_A + n) * LANES:(g * HPG_A + n + 1) * LANES].astype(BF16)
                              for n in range(HPG_A)], axis=0)
        s = _nt(q3, kv_ref[0, :, g * LANES:(g + 1) * LANES]) * (HEAD_DIM ** -0.5)
        p, den = _softmax_parts(s, maskf3)
        o3 = jnp.dot(p.astype(BF16), kv_ref[0, :, (KVH_A + g) * LANES:(KVH_A + g + 1) * LANES],
                     preferred_element_type=F32) / jnp.maximum(den, 1e-30)
        for n in range(HPG_A):
            sl = slice((g * HPG_A + n) * LANES, (g * HPG_A + n + 1) * LANES)
            o_ref[:, sl] = (o3[n * tq:(n + 1) * tq] * _silu(z_ref[:, sl])).astype(o_ref.dtype)


def _dsa_prompt_call(u, post, b, t):
    n_keys = min(TOPK_MAX, t // 4)
    outs = []
    for q0, nqb, klen in _bands(t):
        rows = functools.partial(_band_rows, b, t, q0)
        ospec, oshape = _band_out(b, nqb, 768)
        outs.append(pl.pallas_call(
            functools.partial(_dsa_prompt_kernel, n_keys=n_keys, q0=q0),
            grid=(b, nqb),
            in_specs=[
                rows(768), rows(1024),
                rows(128, _OFF["a_wi"] // 128), rows(768, _OFF["a_z"] // 768),
                pl.BlockSpec((1, klen, 128), lambda bi, i: (bi, 0, 0)),
                pl.BlockSpec((1, klen, 512), lambda bi, i: (bi, 0, 0)),
            ],
            out_specs=ospec, out_shape=oshape,
            scratch_shapes=[pltpu.VMEM((PROMPT_TQ, klen), I32), pltpu.VMEM((klen, PROMPT_TQ), I32),
                            pltpu.VMEM((PROMPT_TQ, klen), F32)],
            compiler_params=_params(("parallel", "arbitrary")),
            name="dsa_prompt",
        )(post["qa"], post["qi"], u, u, post["kidx_dup"].reshape(b, t, 128), post["a_kv_h"].reshape(b, t, 512)))
    return jnp.concatenate(outs, axis=1).reshape(b * t, 768)


def _diff_prompt_kernel(q_ref, z_ref, kv_ref, lam_ref, g_ref, o_ref, *, layer, q0):
    tq = q_ref.shape[0]
    t = kv_ref.shape[1]
    qpos = (q0 + pl.program_id(1)) * tq + lax.broadcasted_iota(I32, (tq, 1), 0)
    kpos = lax.broadcasted_iota(I32, (1, t), 1)
    maskf = jnp.where(kpos <= qpos, 1.0, 0.0)
    maskf2 = jnp.concatenate([maskf, maskf], axis=0)
    lane = lax.broadcasted_iota(I32, (1, LANES), 1)
    lo = lane < HALF_DIM
    lam, lam_init = _lam_value(lam_ref, layer)
    for h in range(H_B):
        sl = slice(h * LANES, (h + 1) * LANES)
        q = q_ref[:, sl]
        qm = jnp.concatenate([jnp.where(lo, q, 0.0), jnp.where(lo, 0.0, q)], axis=0).astype(BF16)
        p, den = _softmax_parts(_nt(qm, kv_ref[0, :, sl]) * (HALF_DIM ** -0.5), maskf2)
        pr = p / jnp.maximum(den, 1e-30)
        a = pr[0:tq] - lam * pr[tq:2 * tq]
        o = jnp.dot(a.astype(BF16), kv_ref[0, :, (H_B + h) * LANES:(H_B + h + 1) * LANES],
                    preferred_element_type=F32)
        o = _subln(o, g_ref[...], lam_init)
        o_ref[:, sl] = (o * _silu(z_ref[:, sl])).astype(o_ref.dtype)


def _diff_prompt_call(u, post, lam_l, subln_l, layer, b, t):
    outs = []
    for q0, nqb, klen in _bands(t):
        rows = functools.partial(_band_rows, b, t, q0)
        ospec, oshape = _band_out(b, nqb, 512)
        outs.append(pl.pallas_call(
            functools.partial(_diff_prompt_kernel, layer=layer, q0=q0),
            grid=(b, nqb),
            in_specs=[
                rows(512), rows(512, _OFF["b_z"] // 512),
                pl.BlockSpec((1, klen, 1024), lambda bi, i: (bi, 0, 0)),
                pl.BlockSpec((4, HALF_DIM), lambda bi, i: (0, 0)),
                pl.BlockSpec((1, HEAD_DIM), lambda bi, i: (0, 0)),
            ],
            out_specs=ospec, out_shape=oshape,
            compiler_params=_params(("parallel", "arbitrary")),
            name="diff_prompt",
        )(post["qb"], u, post["b_kv_h"].reshape(b, t, 1024), lam_l, subln_l.reshape(1, HEAD_DIM)))
    return jnp.concatenate(outs, axis=1).reshape(b * t, 512)


def _compress_rows(rows_ref, first_row, a, pe_term, c):
    sl = slice(c * LANES, (c + 1) * LANES)
    blk = rows_ref[pl.ds(first_row + c, CMP_BLOCK, stride=SLABS_C), :]
    return jnp.sum(blk * a[:, sl], axis=0, keepdims=True) + pe_term[:, sl]


def _compress_kernel(rows_ref, a_ref, pe_ref, o_ref):
    nb = o_ref.shape[1]
    a = a_ref[...]
    pe_term = jnp.sum(pe_ref[...] * a, axis=0, keepdims=True)
    for j in range(nb):
        for c in range(SLABS_C):
            o_ref[0, j:j + 1, c * LANES:(c + 1) * LANES] = _compress_rows(
                rows_ref, j * CMP_BLOCK * SLABS_C, a, pe_term, c)


def _compress_call(cmp_rows, alpha_e, pe_e, layer, b, t):
    nb = t // CMP_BLOCK
    return pl.pallas_call(
        _compress_kernel,
        grid=(b,),
        in_specs=[
            pl.BlockSpec((None, t * SLABS_C, LANES), lambda bi: (layer, bi, 0)),
            pl.BlockSpec((CMP_BLOCK, 512), lambda bi: (0, 0)),
            pl.BlockSpec((CMP_BLOCK, 512), lambda bi: (0, 0)),
        ],
        out_specs=pl.BlockSpec((1, nb, 512), lambda bi: (bi, 0, 0)),
        out_shape=jax.ShapeDtypeStruct((b, nb, 512), F32),
        compiler_params=_params(("parallel",)),
        name="compress",
    )(cmp_rows, alpha_e, pe_e)


def _rep3(x):
    return jnp.concatenate([x] * HPG_C, axis=0)


def _nsa_prompt_kernel(q_ref, z_ref, gt_ref, cmp_ref, slc_ref, *rest, q0, win_starts):
    win_refs, o_ref = rest[:-1], rest[-1]
    tq = q_ref.shape[0]
    t = slc_ref.shape[1]
    nb = cmp_ref.shape[1]
    qpos = (q0 + pl.program_id(1)) * tq + lax.broadcasted_iota(I32, (tq, 1), 0)
    kpos = lax.broadcasted_iota(I32, (1, t), 1)
    causal = kpos <= qpos
    wpos = jnp.concatenate([s + lax.broadcasted_iota(I32, (1, r.shape[1]), 1) for r, s in zip(win_refs, win_starts)],
                           axis=1)
    winf3 = _rep3(jnp.where((wpos <= qpos) & (wpos > qpos - WINDOW), 1.0, 0.0))
    blk = lax.broadcasted_iota(I32, (1, nb), 1)
    cmpf3 = _rep3(jnp.where((blk + 1) * CMP_BLOCK - 1 <= qpos, 1.0, 0.0))
    cur = qpos // CMP_BLOCK
    expand = jnp.where(lax.broadcasted_iota(I32, (nb, t), 1) // CMP_BLOCK == lax.broadcasted_iota(I32, (nb, t), 0),
                       1.0, 0.0).astype(BF16)
    gates = jax.nn.sigmoid(gt_ref[...])
    scale = HEAD_DIM ** -0.5
    cmpkv = cmp_ref[0].astype(BF16)
    for g in range(KVH_C):
        ksl = slice(g * LANES, (g + 1) * LANES)
        vsl = slice((KVH_C + g) * LANES, (KVH_C + g + 1) * LANES)
        q3 = jnp.concatenate([q_ref[:, (g * HPG_C + n) * LANES:(g * HPG_C + n + 1) * LANES].astype(BF16)
                              for n in range(HPG_C)], axis=0)
        p, den = _softmax_parts(_nt(q3, cmpkv[:, ksl]) * scale, cmpf3)
        pc = p / jnp.maximum(den, 1e-30)
        o_cmp = jnp.dot(pc.astype(BF16), cmpkv[:, vsl], preferred_element_type=F32)
        imp = pc[0:tq] + pc[tq:2 * tq] + pc[2 * tq:3 * tq]
        sel = _top_blocks(_block_importance(imp, cur, nb), min(SEL_BLOCKS, nb), nb)
        slcf3 = _rep3(jnp.where(causal, jnp.dot(sel.astype(BF16), expand, preferred_element_type=F32), 0.0))
        p, den = _softmax_parts(_nt(q3, slc_ref[0, :, ksl]) * scale, slcf3)
        o_slc = jnp.dot(p.astype(BF16), slc_ref[0, :, vsl], preferred_element_type=F32) / jnp.maximum(den, 1e-30)
        s = jnp.concatenate([_nt(q3, r[0, :, ksl]) for r in win_refs], axis=1) * scale
        p, den = _softmax_parts(s, winf3)
        ph = p.astype(BF16)
        o_win, off = 0.0, 0
        for r in win_refs:
            o_win = o_win + jnp.dot(ph[:, off:off + r.shape[1]], r[0, :, vsl], preferred_element_type=F32)
            off += r.shape[1]
        o_win = o_win / jnp.maximum(den, 1e-30)
        for n in range(HPG_C):
            h = g * HPG_C + n
            sl = slice(h * LANES, (h + 1) * LANES)
            rs = slice(n * tq, (n + 1) * tq)
            o = (o_cmp[rs] * gates[:, 3 * h:3 * h + 1] + o_slc[rs] * gates[:, 3 * h + 1:3 * h + 2]
                 + o_win[rs] * gates[:, 3 * h + 2:3 * h + 3])
            o_ref[:, sl] = (o * _silu(z_ref[:, sl])).astype(o_ref.dtype)


def _nsa_prompt_call(u, post, cmpkv, b, t):
    outs = []
    win_h = post["win_h"].reshape(b, t, 512)
    for band, (q0, nqb, klen) in enumerate(_bands(t)):
        rows = functools.partial(_band_rows, b, t, q0)
        ospec, oshape = _band_out(b, nqb, 768)
        span = nqb * PROMPT_TQ
        if span >= WINDOW and q0 * PROMPT_TQ == band * span:
            blocks = [band - 1, band] if band > 0 else [band]
            win_specs = [pl.BlockSpec((1, span, 512), functools.partial(lambda bi, i, k: (bi, k, 0), k=k))
                         for k in blocks]
            win_starts = tuple(k * span for k in blocks)
        else:
            win_specs = [pl.BlockSpec((1, klen, 512), lambda bi, i: (bi, 0, 0))]
            win_starts = (0,)
        outs.append(pl.pallas_call(
            functools.partial(_nsa_prompt_kernel, q0=q0, win_starts=win_starts),
            grid=(b, nqb),
            in_specs=[
                rows(768), rows(768, _OFF["c_z"] // 768), rows(128, _OFF["c_g"] // 128),
                pl.BlockSpec((1, klen // CMP_BLOCK, 512), lambda bi, i: (bi, 0, 0)),
                pl.BlockSpec((1, klen, 512), lambda bi, i: (bi, 0, 0)),
            ] + win_specs,
            out_specs=ospec, out_shape=oshape,
            compiler_params=_params(("parallel", "arbitrary")),
            name="nsa_prompt",
        )(post["qc"], u, u, cmpkv, post["slc_h"].reshape(b, t, 512), *([win_h] * len(win_specs))))
    return jnp.concatenate(outs, axis=1).reshape(b * t, 768)


def _page_specs(n_pages, rows, layer):
    return [pl.BlockSpec((None, None, rows, LANES), functools.partial(_page_index, layer=layer, page=p))
            for p in range(n_pages)]


def _page_index(b, pt_ref, *, layer, page):
    return (layer, pt_ref[b, page], 0, 0)


def _seq_spec(*shape):
    nd = len(shape)
    return pl.BlockSpec((1,) + shape, lambda b, pt: (b,) + (0,) * nd)


def _dsa_score_kernel(pt_ref, qi_ref, wi_ref, kin_ref, *rest):
    del pt_ref
    pages, o_ref = rest[:-1], rest[-1]
    qi = qi_ref[0]
    w = wi_ref[0] * (IDX_HEADS ** -0.5)
    qh = qi.astype(BF16)
    for p, page in enumerate(pages):
        s = jnp.dot(qh, page[...].astype(BF16), preferred_element_type=F32)
        rel = jnp.maximum(s * (IDX_DIM ** -0.5), 0.0) * w
        o_ref[0, :, p * PAGE_SIZE:(p + 1) * PAGE_SIZE] = jnp.sum(rel, axis=0, keepdims=True)
    s_new = jnp.sum(qi * kin_ref[0], axis=-1, keepdims=True)
    sc = jnp.sum(jnp.maximum(s_new * (IDX_DIM ** -0.5), 0.0) * w, axis=0, keepdims=True)
    lane = lax.broadcasted_iota(I32, (1, LANES), 1)
    n_past = len(pages) * PAGE_SIZE
    o_ref[0, :, n_past:n_past + LANES] = jnp.where(lane == 0, sc, -jnp.inf)


def _dsa_score_call(qi, wi, ki_new, cache_kidx_t, page_table, layer):
    bs, n_pages = page_table.shape
    lk = n_pages * PAGE_SIZE + LANES
    grid_spec = pltpu.PrefetchScalarGridSpec(
        num_scalar_prefetch=1, grid=(bs,),
        in_specs=[_seq_spec(IDX_HEADS, IDX_DIM), _seq_spec(IDX_HEADS, 1), _seq_spec(1, IDX_DIM)]
        + _page_specs(n_pages, IDX_DIM, layer),
        out_specs=_seq_spec(1, lk))
    return pl.pallas_call(
        _dsa_score_kernel, grid_spec=grid_spec,
        out_shape=jax.ShapeDtypeStruct((bs, 1, lk), F32),
        compiler_params=_params(("arbitrary",)),
        name="dsa_score",
    )(page_table, qi.reshape(bs, IDX_HEADS, IDX_DIM), wi.reshape(bs, IDX_HEADS, 1),
      ki_new.reshape(bs, 1, IDX_DIM), *([cache_kidx_t] * n_pages))


def _select_kernel(s_ref, o_ref, key_ref, keyt_ref, *, n_valid, n_keys):
    r, l = s_ref.shape
    kpos = lax.broadcasted_iota(I32, (1, l), 1)
    _topk_mask(s_ref[...], jnp.broadcast_to(kpos < n_valid, (r, l)), n_keys, key_ref, keyt_ref, o_ref)


def _select_call(scores, n_valid, n_keys):
    r, l = scores.shape
    return pl.pallas_call(
        functools.partial(_select_kernel, n_valid=n_valid, n_keys=n_keys),
        grid=(1,),
        in_specs=[pl.BlockSpec((r, l), lambda i: (0, 0))],
        out_specs=pl.BlockSpec((r, l), lambda i: (0, 0)),
        out_shape=jax.ShapeDtypeStruct((r, l), F32),
        scratch_shapes=[pltpu.VMEM((r, l), I32), pltpu.VMEM((l, r), I32)],
        compiler_params=_params(("arbitrary",)),
        name="dsa_select",
    )(scores)


def _decode_attend(q, maskf, mask_new, pages, new_ref, kslab, n_slabs, scale):
    qh = q.astype(BF16)
    s = jnp.concatenate([_nt(qh, _slab(pg, kslab, n_slabs).astype(BF16)) for pg in pages], axis=1) * scale
    s_new = jnp.sum(q * new_ref[kslab:kslab + 1, :], axis=-1, keepdims=True) * scale
    sm = jnp.where(maskf > 0.0, s, NEG)
    sn = jnp.where(mask_new > 0.0, s_new, NEG)
    m = jnp.maximum(jnp.max(sm, axis=-1, keepdims=True), sn)
    p = jnp.exp(sm - m) * maskf
    p_new = jnp.exp(sn - m) * mask_new
    den = jnp.sum(p, axis=-1, keepdims=True) + p_new
    return p, p_new, den


def _decode_pv(p, p_new, pages, new_ref, vslab, n_slabs):
    ph = p.astype(BF16)
    o = p_new * new_ref[vslab:vslab + 1, :]
    off = 0
    for pg in pages:
        tokens = pg.shape[0] // n_slabs
        o = o + jnp.dot(ph[:, off:off + tokens], _slab(pg, vslab, n_slabs).astype(BF16),
                        preferred_element_type=F32)
        off += tokens
    return o


def _dsa_attn_kernel(pt_ref, q_ref, z_ref, m_ref, new_ref, *rest):
    del pt_ref
    pages, o_ref = rest[:-1], rest[-1]
    n_past = len(pages) * PAGE_SIZE
    mrow = m_ref[0]
    maskf = mrow[:, :n_past]
    mask_new = mrow[:, n_past:n_past + 1]
    new = new_ref.at[0]
    for g in range(KVH_A):
        hs = slice(g * HPG_A, (g + 1) * HPG_A)
        qg = q_ref[0, hs, :]
        p, p_new, den = _decode_attend(qg, maskf, mask_new, pages, new, g, SLABS_A, HEAD_DIM ** -0.5)
        o = _decode_pv(p, p_new, pages, new, KVH_A + g, SLABS_A) / jnp.maximum(den, 1e-30)
        o_ref[0, hs, :] = (o * _silu(z_ref[0, hs, :])).astype(o_ref.dtype)


def _dsa_attn_call(q, z, maskf, kv_new, cache_kv, page_table, layer):
    bs, n_pages = page_table.shape
    lk = maskf.shape[-1]
    grid_spec = pltpu.PrefetchScalarGridSpec(
        num_scalar_prefetch=1, grid=(bs,),
        in_specs=[_seq_spec(H_A, HEAD_DIM), _seq_spec(H_A, HEAD_DIM), _seq_spec(1, lk), _seq_spec(SLABS_A, LANES)]
        + _page_specs(n_pages, PAGE_SIZE * SLABS_A, layer),
        out_specs=_seq_spec(H_A, HEAD_DIM))
    return pl.pallas_call(
        _dsa_attn_kernel, grid_spec=grid_spec,
        out_shape=jax.ShapeDtypeStruct((bs, H_A, HEAD_DIM), BF16),
        compiler_params=_params(("arbitrary",)),
        name="dsa_attn",
    )(page_table, q.reshape(bs, H_A, HEAD_DIM), z.reshape(bs, H_A, HEAD_DIM), maskf.reshape(bs, 1, lk),
      kv_new.reshape(bs, SLABS_A, LANES), *([cache_kv] * n_pages)).reshape(bs, H_A * HEAD_DIM)


def _diff_dec_kernel(pt_ref, q_ref, z_ref, new_ref, lam_ref, g_ref, *rest, layer):
    del pt_ref
    pages, o_ref = rest[:-1], rest[-1]
    n_past = len(pages) * PAGE_SIZE
    new = new_ref.at[0]
    lam, lam_init = _lam_value(lam_ref, layer)
    lane = lax.broadcasted_iota(I32, (2, LANES), 1)
    rowi = lax.broadcasted_iota(I32, (2, LANES), 0)
    halfsel = (rowi == 0) == (lane < HALF_DIM)
    ones = jnp.ones((1, n_past), F32)
    one = jnp.ones((1, 1), F32)
    for h in range(H_B):
        qm = jnp.where(halfsel, jnp.broadcast_to(q_ref[0, h:h + 1, :], (2, LANES)), 0.0)
        p, p_new, den = _decode_attend(qm, ones, one, pages, new, h, SLABS_B, HALF_DIM ** -0.5)
        inv = 1.0 / jnp.maximum(den, 1e-30)
        p, p_new = p * inv, p_new * inv
        a = p[0:1] - lam * p[1:2]
        a_new = p_new[0:1] - lam * p_new[1:2]
        o = _subln(_decode_pv(a, a_new, pages, new, H_B + h, SLABS_B), g_ref[...], lam_init)
        o_ref[0, h:h + 1, :] = (o * _silu(z_ref[0, h:h + 1, :])).astype(o_ref.dtype)


def _diff_dec_call(q, z, kv_new, cache_kv, page_table, lam_l, subln_l, layer):
    bs, n_pages = page_table.shape
    grid_spec = pltpu.PrefetchScalarGridSpec(
        num_scalar_prefetch=1, grid=(bs,),
        in_specs=[_seq_spec(H_B, HEAD_DIM), _seq_spec(H_B, HEAD_DIM), _seq_spec(SLABS_B, LANES),
                  pl.BlockSpec((4, HALF_DIM), lambda b, pt: (0, 0)),
                  pl.BlockSpec((1, HEAD_DIM), lambda b, pt: (0, 0))]
        + _page_specs(n_pages, PAGE_SIZE * SLABS_B, layer),
        out_specs=_seq_spec(H_B, HEAD_DIM))
    return pl.pallas_call(
        functools.partial(_diff_dec_kernel, layer=layer), grid_spec=grid_spec,
        out_shape=jax.ShapeDtypeStruct((bs, H_B, HEAD_DIM), BF16),
        compiler_params=_params(("arbitrary",)),
        name="diff_dec",
    )(page_table, q.reshape(bs, H_B, HEAD_DIM), z.reshape(bs, H_B, HEAD_DIM), kv_new.reshape(bs, SLABS_B, LANES),
      lam_l, subln_l.reshape(1, HEAD_DIM), *([cache_kv] * n_pages)).reshape(bs, H_B * HEAD_DIM)


def _nsa_dec_kernel(*refs, n_pages, has_prev):
    (pt_ref, q_ref, z_ref, gt_ref, slcn_ref, winn_ref, a_ref, pe_ref, win_ref) = refs[:9]
    del pt_ref
    cmp_pages = refs[9:9 + n_pages]
    slc_pages = refs[9 + n_pages:9 + 2 * n_pages]
    o_ref, wout_ref, cmp_scr = refs[-3], refs[-2], refs[-1]
    del has_prev
    n_past = n_pages * PAGE_SIZE
    nb = n_past // CMP_BLOCK
    per_page = PAGE_SIZE // CMP_BLOCK
    scale = HEAD_DIM ** -0.5
    gates = jax.nn.sigmoid(gt_ref[0])
    slc_new = slcn_ref.at[0]
    win_new = winn_ref.at[0]
    wrows = win_ref.shape[0]
    wout_ref[pl.ds(0, wrows - SLABS_C), :] = win_ref[pl.ds(SLABS_C, wrows - SLABS_C), :]
    wout_ref[pl.ds(wrows - SLABS_C, SLABS_C), :] = win_new[...]
    a = a_ref[...]
    pe_term = jnp.sum(pe_ref[...] * a, axis=0, keepdims=True)
    cmp_scr[nb:, :] = jnp.zeros((LANES - nb, 512), F32)
    for p, page in enumerate(cmp_pages):
        for j in range(per_page):
            bj = p * per_page + j
            for c in range(SLABS_C):
                cmp_scr[bj:bj + 1, c * LANES:(c + 1) * LANES] = _compress_rows(
                    page, j * CMP_BLOCK * SLABS_C, a, pe_term, c)
    cmph = cmp_scr[...].astype(BF16)
    cur = n_past // CMP_BLOCK
    nblk = cur + 1
    lane = lax.broadcasted_iota(I32, (1, LANES), 1)
    cmpf = jnp.where(lane < nb, 1.0, 0.0)
    wtok = wrows // SLABS_C
    wpos = lax.broadcasted_iota(I32, (1, wtok), 1)
    winf = jnp.where(wpos >= wtok + 1 - WINDOW, 1.0, 0.0)
    one = jnp.ones((1, 1), F32)
    for g in range(KVH_C):
        ksl = slice(g * LANES, (g + 1) * LANES)
        vsl = slice((KVH_C + g) * LANES, (KVH_C + g + 1) * LANES)
        qg = q_ref[0, g * HPG_C:(g + 1) * HPG_C, :]
        p, den = _softmax_parts(_nt(qg.astype(BF16), cmph[:, ksl]) * scale, cmpf)
        pc = p / jnp.maximum(den, 1e-30)
        o_cmp = jnp.dot(pc.astype(BF16), cmph[:, vsl], preferred_element_type=F32)
        imp = jnp.sum(pc, axis=0, keepdims=True)
        sel = _top_blocks(_block_importance(imp, cur, nblk), min(SEL_BLOCKS, nblk), nblk)
        tok = []
        for p in range(n_pages):
            m = jnp.zeros((1, PAGE_SIZE), F32)
            for j in range(per_page):
                bj = p * per_page + j
                m = jnp.where((lane // CMP_BLOCK) == j, sel[:, bj:bj + 1], m)
            tok.append(m)
        slcf = jnp.concatenate(tok, axis=1)
        sel_new = sel[:, cur:cur + 1]
        p, p_new, den = _decode_attend(qg, slcf, sel_new, slc_pages, slc_new, g, SLABS_C, scale)
        o_slc = _decode_pv(p, p_new, slc_pages, slc_new, KVH_C + g, SLABS_C) / jnp.maximum(den, 1e-30)
        p, p_new, den = _decode_attend(qg, winf, one, [win_ref], win_new, g, SLABS_C, scale)
        o_win = _decode_pv(p, p_new, [win_ref], win_new, KVH_C + g, SLABS_C) / jnp.maximum(den, 1e-30)
        for n in range(HPG_C):
            h = g * HPG_C + n
            o = (o_cmp[n:n + 1] * gates[:, 3 * h:3 * h + 1] + o_slc[n:n + 1] * gates[:, 3 * h + 1:3 * h + 2]
                 + o_win[n:n + 1] * gates[:, 3 * h + 2:3 * h + 3])
            o_ref[0, h:h + 1, :] = (o * _silu(z_ref[0, h:h + 1, :])).astype(o_ref.dtype)


def _nsa_dec_call(q, z, gt, slc_new, win_new, alpha_e, pe_e, win_state, cache_cmp, cache_slc, page_table, layer,
                  win_prev):
    bs, n_pages = page_table.shape
    depth, _, wrows, _ = win_state.shape
    in_specs = [_seq_spec(H_C, HEAD_DIM), _seq_spec(H_C, HEAD_DIM), _seq_spec(1, LANES),
                _seq_spec(SLABS_C, LANES), _seq_spec(SLABS_C, LANES),
                pl.BlockSpec((CMP_BLOCK, 512), lambda b, pt: (0, 0)),
                pl.BlockSpec((CMP_BLOCK, 512), lambda b, pt: (0, 0)),
                pl.BlockSpec((None, None, wrows, LANES), lambda b, pt: (layer, b, 0, 0))]
    in_specs += _page_specs(n_pages, PAGE_SIZE * SLABS_C, layer) + _page_specs(n_pages, PAGE_SIZE * SLABS_C, layer)
    args = [page_table, q.reshape(bs, H_C, HEAD_DIM), z.reshape(bs, H_C, HEAD_DIM), gt.reshape(bs, 1, LANES),
            slc_new.reshape(bs, SLABS_C, LANES), win_new.reshape(bs, SLABS_C, LANES), alpha_e, pe_e, win_state]
    args += [cache_cmp] * n_pages + [cache_slc] * n_pages
    aliases = {}
    if win_prev is not None:
        aliases[len(args)] = 1
        args.append(win_prev)
        in_specs.append(pl.BlockSpec(memory_space=pl.ANY))
    grid_spec = pltpu.PrefetchScalarGridSpec(
        num_scalar_prefetch=1, grid=(bs,),
        in_specs=in_specs,
        out_specs=[_seq_spec(H_C, HEAD_DIM),
                   pl.BlockSpec((None, None, wrows, LANES), lambda b, pt: (layer, b, 0, 0))],
        scratch_shapes=[pltpu.VMEM((LANES, 512), F32)])

    def body(*refs):
        if win_prev is not None:
            refs = refs[:9 + 2 * n_pages] + refs[10 + 2 * n_pages:]
        _nsa_dec_kernel(*refs, n_pages=n_pages, has_prev=win_prev is not None)

    y, win_out = pl.pallas_call(
        body, grid_spec=grid_spec,
        out_shape=[jax.ShapeDtypeStruct((bs, H_C, HEAD_DIM), BF16),
                   jax.ShapeDtypeStruct((depth, bs, wrows, LANES), F32)],
        input_output_aliases=aliases,
        compiler_params=_params(("arbitrary",)),
        name="nsa_dec",
    )(*args)
    return y.reshape(bs, H_C * HEAD_DIM), win_out


def _outproj_kernel(ma_ref, mb_ref, mc_ref, w_ref, x_ref, gate_ref, fg_ref, o_ref, *, final):
    wa, wb = H_A * HEAD_DIM, (H_A + H_B) * HEAD_DIM
    out = jnp.dot(ma_ref[...], w_ref[0:wa, :], preferred_element_type=F32)
    out = out + jnp.dot(mb_ref[...], w_ref[wa:wb, :], preferred_element_type=F32)
    out = out + jnp.dot(mc_ref[...], w_ref[wb:, :], preferred_element_type=F32)
    xn = x_ref[...] + gate_ref[...] * out
    if final:
        xn = xn * lax.rsqrt(jnp.mean(xn * xn, axis=-1, keepdims=True) + NORM_EPS) * fg_ref[...]
    o_ref[...] = xn


def _outproj_call(ma, mb, mc, w, x, gate, fg, rows_per_batch, final):
    m, d = x.shape
    tm = min(m, 256)
    if rows_per_batch > 1:
        per = rows_per_batch // tm
        gspec = pl.BlockSpec((None, 1, d), lambda i: (i // per, 0, 0))
    else:
        gspec = pl.BlockSpec((tm, d), lambda i: (i, 0))

    def rows(width):
        return pl.BlockSpec((tm, width), lambda i: (i, 0))

    return pl.pallas_call(
        functools.partial(_outproj_kernel, final=final),
        grid=(m // tm,),
        in_specs=[rows(768), rows(512), rows(768), pl.BlockSpec((d, d), lambda i: (0, 0)), rows(d), gspec,
                  pl.BlockSpec((1, d), lambda i: (0, 0))],
        out_specs=rows(d),
        out_shape=jax.ShapeDtypeStruct((m, d), F32),
        compiler_params=_params(("parallel",)),
        name="outproj",
    )(ma, mb, mc, w, x, gate, fg)


def _pad_w_in(w_in):
    src, o = {}, 0
    for name, width in _IN_SPLITS:
        src[name] = (o, width)
        o += width
    parts = []
    for name, width in _PAD_ORDER:
        if name in src:
            s, sw = src[name]
            parts.append(w_in[..., s:s + sw])
            if width > sw:
                parts.append(jnp.zeros(w_in.shape[:-1] + (width - sw,), w_in.dtype))
        else:
            parts.append(jnp.zeros(w_in.shape[:-1] + (width,), w_in.dtype))
    return jnp.concatenate(parts, axis=-1).astype(BF16)


def _ucol(u, name, width=None):
    width = width or dict(_PAD_ORDER)[name]
    return u[:, _OFF[name]:_OFF[name] + width]


def kernel(x_prompt, x_sample, cache_a_kv, cache_a_kidx, cache_b_kv, cache_c_cmp_kv, cache_c_slc_kv,
           state_c_win_kv, page_table, c_prompt, c_sample, w_mod, b_mod, norm_g, w_in, w_out, lam,
           subln_g, cmp_alpha, cmp_pe, final_g):
    depth, d, _ = w_mod.shape
    b, t, _ = x_prompt.shape
    bs, ts, _ = x_sample.shape
    assert ts == 1 and t % 256 == 0
    n_pool = cache_a_kv.shape[1]
    n_pages = page_table.shape[1]
    n_past = n_pages * PAGE_SIZE
    wbuf = state_c_win_kv.shape[2]
    assert wbuf == WINDOW and n_past >= WINDOW

    w_in_p = _pad_w_in(w_in)
    w_out_h = w_out.astype(BF16)
    mod = _mod_call(jnp.concatenate([c_prompt, c_sample], axis=0), w_mod, b_mod)
    alpha_e = jnp.broadcast_to(cmp_alpha[..., None], cmp_alpha.shape + (HEAD_DIM,)).reshape(depth, CMP_BLOCK, 512)
    pe_e = cmp_pe.reshape(depth, CMP_BLOCK, 512)
    fg = final_g.reshape(1, d)

    ca_kv = cache_a_kv.reshape(depth, n_pool, PAGE_SIZE * SLABS_A, LANES)
    cb_kv = cache_b_kv.reshape(depth, n_pool, PAGE_SIZE * SLABS_B, LANES)
    cc_cmp = cache_c_cmp_kv.reshape(depth, n_pool, PAGE_SIZE * SLABS_C, LANES)
    cc_slc = cache_c_slc_kv.reshape(depth, n_pool, PAGE_SIZE * SLABS_C, LANES)
    win_state = state_c_win_kv.reshape(depth, bs, wbuf * SLABS_C, LANES)
    ca_kidx_t = jnp.swapaxes(cache_a_kidx, 2, 3)

    tabs_p = _rope_tables(jnp.arange(t, dtype=I32))
    tabs_s = _rope_tables(n_past + jnp.arange(1, dtype=I32))

    x = x_prompt.reshape(b * t, d)
    post = None
    for layer in range(depth):
        shift, scale, gate = (mod[layer, :b, i * d:(i + 1) * d].reshape(b, 1, d) for i in range(3))
        u = _inproj_call(x, norm_g[layer].reshape(1, d), scale, shift, w_in_p[layer], t)
        post = _post_call(u, tabs_p, t, layer, depth, post)
        ya = _dsa_prompt_call(u, post, b, t)
        yb = _diff_prompt_call(u, post, lam[layer], subln_g[layer], layer, b, t)
        cmpkv = _compress_call(post["cmp"], alpha_e[layer], pe_e[layer], layer, b, t)
        yc = _nsa_prompt_call(u, post, cmpkv, b, t)
        x = _outproj_call(ya, yb, yc, w_out_h[layer], x, gate, fg, t, layer == depth - 1)
    y_prompt = x.reshape(b, t, d)
    st_p = post

    x = x_sample.reshape(bs, d)
    post = None
    win_out = None
    for layer in range(depth):
        shift, scale, gate = (mod[layer, b:, i * d:(i + 1) * d] for i in range(3))
        u = _inproj_call(x, norm_g[layer].reshape(1, d), scale, shift, w_in_p[layer], 1)
        post = _post_call(u, tabs_s, 1, layer, depth, post)
        n_keys = min(TOPK_MAX, (n_past + 1) // 4)
        scores = _dsa_score_call(post["qi"], _ucol(u, "a_wi", IDX_HEADS), post["a_kidx"][layer], ca_kidx_t,
                                 page_table, layer)
        maskf = _select_call(scores.reshape(bs, -1), n_past + 1, n_keys)
        ya = _dsa_attn_call(post["qa"], _ucol(u, "a_z"), maskf, post["a_kv"][layer], ca_kv, page_table, layer)
        yb = _diff_dec_call(post["qb"], _ucol(u, "b_z"), post["b_kv"][layer], cb_kv, page_table, lam[layer],
                            subln_g[layer], layer)
        yc, win_out = _nsa_dec_call(post["qc"], _ucol(u, "c_z"), _ucol(u, "c_g"), post["slc"][layer],
                                    post["win"][layer], alpha_e[layer], pe_e[layer], win_state, cc_cmp, cc_slc,
                                    page_table, layer, win_out)
        x = _outproj_call(ya, yb, yc, w_out_h[layer], x, gate, fg, 1, layer == depth - 1)
    y_sample = x.reshape(bs, 1, d)
    st_s = post

    outs = []
    for name, tail in (("a_kv", (2, KVH_A, HEAD_DIM)), ("a_kidx", (IDX_DIM,)), ("b_kv", (2, H_B, HEAD_DIM)),
                       ("cmp", (2, KVH_C, HEAD_DIM)), ("slc", (2, KVH_C, HEAD_DIM))):
        outs.append(st_p[name].reshape((depth, b, t) + tail))
        outs.append(st_s[name].reshape((depth, bs, 1) + tail))
    keep = min(WINDOW, t)
    win_p = st_p["win"].reshape(depth, b, t, 2, KVH_C, HEAD_DIM)[:, :, t - keep:]
    win_s = win_out.reshape(depth, bs, wbuf, 2, KVH_C, HEAD_DIM)
    return (y_prompt, y_sample, *outs, win_p, win_s)
```

```python
import functools
import math

import jax
import jax.numpy as jnp
from jax import lax
from jax.experimental import pallas as pl
from jax.experimental.pallas import tpu as pltpu

F32 = jnp.float32
BF16 = jnp.bfloat16
I32 = jnp.int32

HEAD_DIM = 128
H_A, H_B, H_C = 6, 4, 6
KVH_A, KVH_C = 2, 2
HPG_A, HPG_C = H_A // KVH_A, H_C // KVH_C
IDX_HEADS, IDX_DIM = 16, 64
TOPK_MAX = 256
HALF_DIM = 64
SUBLN_EPS = 1e-5
CMP_BLOCK = 64
SEL_BLOCKS = 8
WINDOW = 512
ROPE_THETA = 10000.0
NORM_EPS = 1e-6
PAGE_SIZE = 128

LANES = 128
VMEM_LIMIT = 56 * 1024 * 1024

NEG = -1e30
INT_MIN = -2147483648
SLABS_A = 2 * KVH_A
SLABS_B = 2 * H_B
SLABS_C = 2 * KVH_C
PROMPT_BANDS = 4

_IN_SPLITS = (
    ("a_q", 768), ("a_k", 256), ("a_v", 256), ("a_qi", 1024), ("a_ki", 64), ("a_wi", 16), ("a_z", 768),
    ("b_q", 512), ("b_k", 512), ("b_v", 512), ("b_z", 512),
    ("c_q", 768), ("c_kc", 256), ("c_vc", 256), ("c_ks", 256), ("c_vs", 256), ("c_kw", 256),
    ("c_vw", 256), ("c_g", 18), ("c_z", 768),
)
_PAD_ORDER = (
    ("a_q", 768), ("a_z", 768), ("c_q", 768), ("c_z", 768), ("a_qi", 1024),
    ("b_k", 512), ("b_v", 512), ("b_q", 512), ("b_z", 512),
    ("a_k", 256), ("a_v", 256), ("c_kc", 256), ("c_vc", 256), ("c_ks", 256), ("c_vs", 256),
    ("c_kw", 256), ("c_vw", 256), ("a_ki", 128), ("a_wi", 128), ("c_g", 128), ("_pad", 128),
)
_OFF = {}
_o = 0
for _n, _w in _PAD_ORDER:
    _OFF[_n] = _o
    _o += _w
NP = _o
IN_TN = 512


def _params(sem):
    return pltpu.CompilerParams(dimension_semantics=sem, vmem_limit_bytes=VMEM_LIMIT)


def _nt(a, b):
    return lax.dot_general(a, b, (((1,), (1,)), ((), ())), preferred_element_type=F32)


def _silu(z):
    return z * jax.nn.sigmoid(z)


def _slab(ref, c, n_slabs):
    return ref[pl.ds(c, ref.shape[0] // n_slabs, stride=n_slabs), :]


def _mod_kernel(c_ref, w_ref, b_ref, o_ref):
    c = c_ref[...]
    o_ref[0] = jnp.dot(_silu(c), w_ref[0], preferred_element_type=F32,
                       precision=lax.Precision.HIGHEST) + b_ref[0]


def _mod_call(c_all, w_mod, b_mod):
    depth, d, n3 = w_mod.shape
    bc = c_all.shape[0]
    tn = 768
    return pl.pallas_call(
        _mod_kernel,
        grid=(depth, n3 // tn),
        in_specs=[
            pl.BlockSpec((bc, d), lambda l, j: (0, 0)),
            pl.BlockSpec((1, d, tn), lambda l, j: (l, 0, j)),
            pl.BlockSpec((1, 1, tn), lambda l, j: (l, 0, j)),
        ],
        out_specs=pl.BlockSpec((1, bc, tn), lambda l, j: (l, 0, j)),
        out_shape=jax.ShapeDtypeStruct((depth, bc, n3), F32),
        compiler_params=_params(("parallel", "parallel")),
        name="mod",
    )(c_all, w_mod, b_mod.reshape(depth, 1, n3))


def _inproj_kernel(x_ref, g_ref, sc_ref, sh_ref, w_ref, o_ref, h_ref):
    @pl.when(pl.program_id(1) == 0)
    def _():
        x = x_ref[...]
        y = x * lax.rsqrt(jnp.mean(x * x, axis=-1, keepdims=True) + NORM_EPS) * g_ref[...]
        h_ref[...] = (y * (1.0 + sc_ref[...]) + sh_ref[...]).astype(BF16)

    o_ref[...] = jnp.dot(h_ref[...], w_ref[...], preferred_element_type=F32)


def _inproj_call(x, g, scale, shift, w, layer, rows_per_batch):
    m, d = x.shape
    tm = min(m, 1024)
    if rows_per_batch > 1:
        per = rows_per_batch // tm
        mspec = pl.BlockSpec((None, 1, d), lambda i, j: (i // per, 0, 0))
    else:
        mspec = pl.BlockSpec((tm, d), lambda i, j: (i, 0))
    return pl.pallas_call(
        _inproj_kernel,
        grid=(m // tm, NP // IN_TN),
        in_specs=[
            pl.BlockSpec((tm, d), lambda i, j: (i, 0)),
            pl.BlockSpec((1, d), lambda i, j: (0, 0)),
            mspec, mspec,
            pl.BlockSpec((None, d, IN_TN), lambda i, j: (layer, 0, j)),
        ],
        out_specs=pl.BlockSpec((tm, IN_TN), lambda i, j: (i, j)),
        out_shape=jax.ShapeDtypeStruct((m, NP), F32),
        scratch_shapes=[pltpu.VMEM((tm, d), BF16)],
        compiler_params=_params(("parallel", "arbitrary")),
        name="inproj",
    )(x, g, scale, shift, w)


def _rope128(x, cos, sin_s):
    return x * cos + pltpu.roll(x, 64, 1) * sin_s


def _rope64(x, cos, sin_s, low32):
    partner = jnp.where(low32, pltpu.roll(x, 96, 1), pltpu.roll(x, 32, 1))
    return x * cos + partner * sin_s


_POST_STATE = ("a_kv", "b_kv", "cmp", "slc", "win", "a_kidx")


def _post_kernel(*refs, n_alias):
    (c128_ref, s128_ref, c64_ref, s64_ref,
     uqa_ref, uqc_ref, uqi_ref, ubkv_ref, ubq_ref, uakv_ref, ucmp_ref, uslc_ref, uwin_ref, uki_ref) = refs[:14]
    (qa_ref, qc_ref, qi_ref, qb_ref, akv_ref, bkv_ref, cmp_ref, slc_ref, win_ref, ki_ref,
     bkvh_ref, akvh_ref, slch_ref, winh_ref, kid_ref) = refs[14 + n_alias:]
    c128, s128 = c128_ref[...], s128_ref[...]
    c64, s64 = c64_ref[...], s64_ref[...]
    lane = lax.broadcasted_iota(I32, (1, LANES), 1)
    low32 = (lane % 64) < 32
    tq = uqa_ref.shape[0]

    def rot128(x):
        return _rope128(x, c128, s128)

    def rot64(x):
        return _rope64(x, c64, s64, low32)

    def queries(src, dst, n, rot):
        for h in range(n):
            sl = slice(h * LANES, (h + 1) * LANES)
            dst[:, sl] = rot(src[:, sl])

    def cache_rows(src, dst, dsth, n_slabs, rot):
        for c in range(n_slabs):
            sl = slice(c * LANES, (c + 1) * LANES)
            val = rot(src[:, sl]) if c < n_slabs // 2 else src[:, sl]
            dst[pl.ds(c, tq, stride=n_slabs), :] = val
            if dsth is not None:
                dsth[:, sl] = val.astype(BF16)

    queries(uqa_ref, qa_ref, H_A, rot128)
    queries(uqc_ref, qc_ref, H_C, rot128)
    queries(uqi_ref, qi_ref, IDX_HEADS // 2, rot64)
    queries(ubq_ref, qb_ref, H_B, rot64)
    cache_rows(ubkv_ref, bkv_ref, bkvh_ref, SLABS_B, rot64)
    cache_rows(uakv_ref, akv_ref, akvh_ref, SLABS_A, rot128)
    cache_rows(ucmp_ref, cmp_ref, None, SLABS_C, rot128)
    cache_rows(uslc_ref, slc_ref, slch_ref, SLABS_C, rot128)
    cache_rows(uwin_ref, win_ref, winh_ref, SLABS_C, rot128)
    ki = rot64(uki_ref[...])
    ki_ref[...] = ki[:, :IDX_DIM]
    kid_ref[...] = (ki + pltpu.roll(ki, 64, 1)).astype(BF16)


def _post_call(u, tabs, rows_per_batch, layer, depth, prev):
    m = u.shape[0]
    tq = min(m, 256)
    if rows_per_batch > 1:
        per = rows_per_batch // tq
        tspec = pl.BlockSpec((tq, LANES), lambda i: (i % per, 0))
    else:
        tspec = pl.BlockSpec((1, LANES), lambda i: (0, 0))

    def col(name, width):
        blk = _OFF[name] // width
        assert blk * width == _OFF[name]
        return pl.BlockSpec((tq, width), lambda i: (i, blk))

    def row(width):
        return pl.BlockSpec((tq, width), lambda i: (i, 0))

    def state(rows_per_token, width=LANES):
        return pl.BlockSpec((None, tq * rows_per_token, width), lambda i: (layer, i, 0))

    in_specs = [tspec] * 4 + [
        col("a_q", 768), col("c_q", 768), col("a_qi", 1024), col("b_k", 1024), col("b_q", 512),
        col("a_k", 512), col("c_kc", 512), col("c_ks", 512), col("c_kw", 512), col("a_ki", 128),
    ]
    args = list(tabs) + [u] * 10
    aliases = {}
    if prev is not None:
        for k, name in enumerate(_POST_STATE):
            aliases[len(args)] = 4 + k
            args.append(prev[name])
            in_specs.append(pl.BlockSpec(memory_space=pl.ANY))
    qouts = [(768, F32), (768, F32), (1024, F32), (512, F32)]
    houts = [(1024, BF16), (512, BF16), (512, BF16), (512, BF16), (128, BF16)]
    souts = [(SLABS_A, LANES), (SLABS_B, LANES), (SLABS_C, LANES), (SLABS_C, LANES), (SLABS_C, LANES), (1, IDX_DIM)]
    res = pl.pallas_call(
        functools.partial(_post_kernel, n_alias=len(aliases)),
        grid=(m // tq,),
        in_specs=in_specs,
        out_specs=[row(w) for w, _ in qouts] + [state(r, w) for r, w in souts] + [row(w) for w, _ in houts],
        out_shape=[jax.ShapeDtypeStruct((m, w), dt) for w, dt in qouts]
        + [jax.ShapeDtypeStruct((depth, m * r, w), F32) for r, w in souts]
        + [jax.ShapeDtypeStruct((m, w), dt) for w, dt in houts],
        input_output_aliases=aliases,
        compiler_params=_params(("parallel",)),
        name="post",
    )(*args)
    names = ("qa", "qc", "qi", "qb") + _POST_STATE + ("b_kv_h", "a_kv_h", "slc_h", "win_h", "kidx_dup")
    return dict(zip(names, res))


def _rope_tables(pos):
    def tab(d):
        inv = ROPE_THETA ** (-jnp.arange(0, d, 2, dtype=F32) / d)
        ang = pos.astype(F32)[:, None] * inv[None, :]
        cos, sin = jnp.cos(ang), jnp.sin(ang)
        reps = LANES // d
        return jnp.tile(jnp.concatenate([cos, cos], -1), (1, reps)), jnp.tile(jnp.concatenate([-sin, sin], -1), (1, reps))

    c128, s128 = tab(HEAD_DIM)
    c64, s64 = tab(IDX_DIM)
    return c128, s128, c64, s64


LOG2E = 1.4426950408889634


def _softmax_parts(s, maskf, scale):
    sm = jnp.where(maskf > 0.0, s, NEG)
    m = jnp.max(sm, axis=-1, keepdims=True)
    p = jnp.exp2((sm - m) * (scale * LOG2E)) * maskf
    return p, jnp.sum(p, axis=-1, keepdims=True)


def _row_to_col(x):
    return jnp.broadcast_to(x, (LANES, x.shape[1])).T[:, 0:1]


def _topk_mask(score, valid, k, key_ref, keyt_ref, mask_ref):
    r, l = score.shape
    nchunk = l // LANES
    bits = lax.bitcast_convert_type(score + 0.0, I32)
    key = bits ^ ((bits >> 31) & 0x7FFFFFFF)
    key_ref[...] = jnp.where(valid, key, INT_MIN)
    for c in range(nchunk):
        sl = slice(c * LANES, (c + 1) * LANES)
        keyt_ref[sl, :] = lax.bitcast_convert_type(lax.bitcast_convert_type(key_ref[:, sl], F32).T, I32)
    kf = float(k)

    def count(cand, strict):
        acc = jnp.zeros((LANES, r), F32)
        for c in range(nchunk):
            kc = keyt_ref[c * LANES:(c + 1) * LANES, :]
            acc = acc + jnp.where((kc > cand) if strict else (kc >= cand), 1.0, 0.0)
        return jnp.sum(acc, axis=0, keepdims=True)

    def body(it, t_u):
        cand_u = t_u | lax.shift_left(jnp.int32(1), 31 - it)
        return jnp.where(count(cand_u ^ INT_MIN, False) >= kf, cand_u, t_u)

    t_u = lax.fori_loop(0, 32, body, jnp.zeros((1, r), I32))
    thr_t = t_u ^ INT_MIN
    need_t = kf - count(thr_t, True)
    tied_t = jnp.where(count(thr_t, False) > kf, jnp.where(thr_t != INT_MIN, 1.0, 0.0), 0.0)
    thr = lax.bitcast_convert_type(_row_to_col(lax.bitcast_convert_type(thr_t, F32)), I32)
    has_tie = jnp.max(tied_t) > 0.0

    @pl.when(jnp.logical_not(has_tie))
    def _():
        for c in range(nchunk):
            sl = slice(c * LANES, (c + 1) * LANES)
            kc = key_ref[:, sl]
            mask_ref[:, sl] = jnp.where(kc >= thr, jnp.where(kc != INT_MIN, 1.0, 0.0), 0.0)

    @pl.when(has_tie)
    def _():
        need = _row_to_col(need_t)
        ri = lax.broadcasted_iota(I32, (LANES, LANES), 0)
        ci = lax.broadcasted_iota(I32, (LANES, LANES), 1)
        tri = jnp.where(ri < ci, 1.0, 0.0).astype(BF16)
        carry = jnp.zeros((r, 1), F32)
        for c in range(nchunk):
            sl = slice(c * LANES, (c + 1) * LANES)
            kc = key_ref[:, sl]
            eq = jnp.where(kc == thr, 1.0, 0.0)
            pre = jnp.dot(eq.astype(BF16), tri, preferred_element_type=F32) + carry
            take = jnp.where(kc > thr, 1.0, jnp.where(pre < need, eq, 0.0))
            mask_ref[:, sl] = jnp.where(kc != INT_MIN, take, 0.0)
            carry = carry + jnp.sum(eq, axis=-1, keepdims=True)


def _top_blocks(imp, nsel, nblk):
    lane = lax.broadcasted_iota(I32, (1, imp.shape[1]), 1)
    rank = jnp.zeros(imp.shape, F32)
    for i in range(nblk):
        col = imp[:, i:i + 1]
        later = jnp.where(lane > i, 1.0, 0.0)
        rank = rank + jnp.where(col > imp, 1.0, jnp.where(col == imp, later, 0.0))
    return jnp.where(rank < float(nsel), jnp.where(imp > -jnp.inf, 1.0, 0.0), 0.0)


def _block_importance(imp, cur, nblk):
    blk = lax.broadcasted_iota(I32, imp.shape, 1)
    forced = (blk == 0) | (blk == cur) | (blk == cur - 1)
    imp = jnp.where(forced, jnp.inf, imp)
    return jnp.where((blk <= cur) & (blk < nblk), imp, -jnp.inf)


def _lam_value(lam_ref, layer):
    lp = lam_ref[...]
    a = jnp.sum(lp[0:1] * lp[1:2], axis=-1, keepdims=True)
    b = jnp.sum(lp[2:3] * lp[3:4], axis=-1, keepdims=True)
    lam_init = 0.8 - 0.6 * math.exp(-0.3 * layer)
    return jnp.exp(a) - jnp.exp(b) + lam_init, lam_init


def _subln(o, g, lam_init):
    y = o * lax.rsqrt(jnp.mean(o * o, axis=-1, keepdims=True) + SUBLN_EPS)
    return y * g * (1.0 - lam_init)


PROMPT_TQ = 128


def _bands(t):
    nq = t // PROMPT_TQ
    nb = PROMPT_BANDS if nq % PROMPT_BANDS == 0 else 1
    per = nq // nb
    return [(i * per, per, (i + 1) * per * PROMPT_TQ) for i in range(nb)]


def _band_rows(b, t, q0, width, blk=0):
    nq = t // PROMPT_TQ
    return pl.BlockSpec((PROMPT_TQ, width), lambda bi, i: (bi * nq + q0 + i, blk))


def _band_out(b, nqb, width):
    return (pl.BlockSpec((None, PROMPT_TQ, width), lambda bi, i: (bi, i, 0)),
            jax.ShapeDtypeStruct((b, nqb * PROMPT_TQ, width), BF16))


def _dsa_prompt_kernel(q_ref, qi_ref, wi_ref, z_ref, kid_ref, kv_ref, o_ref, key_ref, keyt_ref, mask_ref, *,
                       n_keys, q0):
    tq = q_ref.shape[0]
    t = kv_ref.shape[1]
    qpos = (q0 + pl.program_id(1)) * tq + lax.broadcasted_iota(I32, (tq, 1), 0)
    kpos = lax.broadcasted_iota(I32, (1, t), 1)
    lane = lax.broadcasted_iota(I32, (1, LANES), 1)
    lo = lane < IDX_DIM
    kd = kid_ref[0]
    w = wi_ref[...] * (IDX_HEADS ** -0.5 * IDX_DIM ** -0.5)
    acc = jnp.zeros((tq, t), F32)
    for pr in range(IDX_HEADS // 2):
        qp = qi_ref[:, pr * LANES:(pr + 1) * LANES]
        for half in range(2):
            qm = jnp.where(lo if half == 0 else jnp.logical_not(lo), qp, 0.0).astype(BF16)
            h = 2 * pr + half
            acc = acc + w[:, h:h + 1] * jnp.maximum(_nt(qm, kd), 0.0)
    _topk_mask(acc, kpos <= qpos, n_keys, key_ref, keyt_ref, mask_ref)
    maskf3 = _rep3(mask_ref[...])
    for g in range(KVH_A):
        q3 = jnp.concatenate([q_ref[:, (g * HPG_A + n) * LANES:(g * HPG_A + n + 1) * LANES].astype(BF16)
                              for n in range(HPG_A)], axis=0)
        p, den = _softmax_parts(_nt(q3, kv_ref[0, :, g * LANES:(g + 1) * LANES]), maskf3, HEAD_DIM ** -0.5)
        o3 = jnp.dot(p.astype(BF16), kv_ref[0, :, (KVH_A + g) * LANES:(KVH_A + g + 1) * LANES],
                     preferred_element_type=F32) / jnp.maximum(den, 1e-30)
        for n in range(HPG_A):
            sl = slice((g * HPG_A + n) * LANES, (g * HPG_A + n + 1) * LANES)
            o_ref[:, sl] = (o3[n * tq:(n + 1) * tq] * _silu(z_ref[:, sl])).astype(o_ref.dtype)


def _dsa_prompt_call(u, post, b, t):
    n_keys = min(TOPK_MAX, t // 4)
    outs = []
    for q0, nqb, klen in _bands(t):
        rows = functools.partial(_band_rows, b, t, q0)
        ospec, oshape = _band_out(b, nqb, 768)
        outs.append(pl.pallas_call(
            functools.partial(_dsa_prompt_kernel, n_keys=n_keys, q0=q0),
            grid=(b, nqb),
            in_specs=[
                rows(768), rows(1024),
                rows(128, _OFF["a_wi"] // 128), rows(768, _OFF["a_z"] // 768),
                pl.BlockSpec((1, klen, 128), lambda bi, i: (bi, 0, 0)),
                pl.BlockSpec((1, klen, 512), lambda bi, i: (bi, 0, 0)),
            ],
            out_specs=ospec, out_shape=oshape,
            scratch_shapes=[pltpu.VMEM((PROMPT_TQ, klen), I32), pltpu.VMEM((klen, PROMPT_TQ), I32),
                            pltpu.VMEM((PROMPT_TQ, klen), F32)],
            compiler_params=_params(("parallel", "arbitrary")),
            name="dsa_prompt",
        )(post["qa"], post["qi"], u, u, post["kidx_dup"].reshape(b, t, 128), post["a_kv_h"].reshape(b, t, 512)))
    return jnp.concatenate(outs, axis=1).reshape(b * t, 768)


def _diff_prompt_kernel(q_ref, z_ref, kv_ref, lam_ref, g_ref, o_ref, *, layer, q0):
    tq = q_ref.shape[0]
    t = kv_ref.shape[1]
    qpos = (q0 + pl.program_id(1)) * tq + lax.broadcasted_iota(I32, (tq, 1), 0)
    kpos = lax.broadcasted_iota(I32, (1, t), 1)
    maskf = jnp.where(kpos <= qpos, 1.0, 0.0)
    maskf2 = jnp.concatenate([maskf, maskf], axis=0)
    lane = lax.broadcasted_iota(I32, (1, LANES), 1)
    lo = lane < HALF_DIM
    lam, lam_init = _lam_value(lam_ref, layer)
    for h in range(H_B):
        sl = slice(h * LANES, (h + 1) * LANES)
        q = q_ref[:, sl]
        qm = jnp.concatenate([jnp.where(lo, q, 0.0), jnp.where(lo, 0.0, q)], axis=0).astype(BF16)
        p, den = _softmax_parts(_nt(qm, kv_ref[0, :, sl]), maskf2, HALF_DIM ** -0.5)
        pr = p / jnp.maximum(den, 1e-30)
        a = pr[0:tq] - lam * pr[tq:2 * tq]
        o = jnp.dot(a.astype(BF16), kv_ref[0, :, (H_B + h) * LANES:(H_B + h + 1) * LANES],
                    preferred_element_type=F32)
        o = _subln(o, g_ref[...], lam_init)
        o_ref[:, sl] = (o * _silu(z_ref[:, sl])).astype(o_ref.dtype)


def _diff_prompt_call(u, post, lam_l, subln_l, layer, b, t):
    outs = []
    for q0, nqb, klen in _bands(t):
        rows = functools.partial(_band_rows, b, t, q0)
        ospec, oshape = _band_out(b, nqb, 512)
        outs.append(pl.pallas_call(
            functools.partial(_diff_prompt_kernel, layer=layer, q0=q0),
            grid=(b, nqb),
            in_specs=[
                rows(512), rows(512, _OFF["b_z"] // 512),
                pl.BlockSpec((1, klen, 1024), lambda bi, i: (bi, 0, 0)),
                pl.BlockSpec((4, HALF_DIM), lambda bi, i: (0, 0)),
                pl.BlockSpec((1, HEAD_DIM), lambda bi, i: (0, 0)),
            ],
            out_specs=ospec, out_shape=oshape,
            compiler_params=_params(("parallel", "arbitrary")),
            name="diff_prompt",
        )(post["qb"], u, post["b_kv_h"].reshape(b, t, 1024), lam_l, subln_l.reshape(1, HEAD_DIM)))
    return jnp.concatenate(outs, axis=1).reshape(b * t, 512)


def _compress_rows(rows_ref, first_row, a, pe_term, c):
    sl = slice(c * LANES, (c + 1) * LANES)
    blk = rows_ref[pl.ds(first_row + c, CMP_BLOCK, stride=SLABS_C), :]
    return jnp.sum(blk * a[:, sl], axis=0, keepdims=True) + pe_term[:, sl]


def _compress_kernel(rows_ref, a_ref, pe_ref, o_ref):
    nb = o_ref.shape[1]
    a = a_ref[...]
    pe_term = jnp.sum(pe_ref[...] * a, axis=0, keepdims=True)
    for j in range(nb):
        for c in range(SLABS_C):
            o_ref[0, j:j + 1, c * LANES:(c + 1) * LANES] = _compress_rows(
                rows_ref, j * CMP_BLOCK * SLABS_C, a, pe_term, c)


def _compress_call(cmp_rows, alpha_e, pe_e, layer, b, t):
    nb = t // CMP_BLOCK
    return pl.pallas_call(
        _compress_kernel,
        grid=(b,),
        in_specs=[
            pl.BlockSpec((None, t * SLABS_C, LANES), lambda bi: (layer, bi, 0)),
            pl.BlockSpec((CMP_BLOCK, 512), lambda bi: (0, 0)),
            pl.BlockSpec((CMP_BLOCK, 512), lambda bi: (0, 0)),
        ],
        out_specs=pl.BlockSpec((1, nb, 512), lambda bi: (bi, 0, 0)),
        out_shape=jax.ShapeDtypeStruct((b, nb, 512), F32),
        compiler_params=_params(("parallel",)),
        name="compress",
    )(cmp_rows, alpha_e, pe_e)


def _rep3(x):
    return jnp.concatenate([x] * HPG_C, axis=0)


def _nsa_prompt_kernel(q_ref, z_ref, gt_ref, cmp_ref, slc_ref, *rest, q0, win_starts):
    win_refs, o_ref = rest[:-1], rest[-1]
    tq = q_ref.shape[0]
    t = slc_ref.shape[1]
    nb = cmp_ref.shape[1]
    qpos = (q0 + pl.program_id(1)) * tq + lax.broadcasted_iota(I32, (tq, 1), 0)
    kpos = lax.broadcasted_iota(I32, (1, t), 1)
    causal = kpos <= qpos
    wpos = jnp.concatenate([s + lax.broadcasted_iota(I32, (1, r.shape[1]), 1) for r, s in zip(win_refs, win_starts)],
                           axis=1)
    winf3 = _rep3(jnp.where((wpos <= qpos) & (wpos > qpos - WINDOW), 1.0, 0.0))
    blk = lax.broadcasted_iota(I32, (1, nb), 1)
    cmpf3 = _rep3(jnp.where((blk + 1) * CMP_BLOCK - 1 <= qpos, 1.0, 0.0))
    cur = qpos // CMP_BLOCK
    expand = jnp.where(lax.broadcasted_iota(I32, (nb, t), 1) // CMP_BLOCK == lax.broadcasted_iota(I32, (nb, t), 0),
                       1.0, 0.0).astype(BF16)
    gates = jax.nn.sigmoid(gt_ref[...])
    scale = HEAD_DIM ** -0.5
    cmpkv = cmp_ref[0].astype(BF16)
    for g in range(KVH_C):
        ksl = slice(g * LANES, (g + 1) * LANES)
        vsl = slice((KVH_C + g) * LANES, (KVH_C + g + 1) * LANES)
        q3 = jnp.concatenate([q_ref[:, (g * HPG_C + n) * LANES:(g * HPG_C + n + 1) * LANES].astype(BF16)
                              for n in range(HPG_C)], axis=0)
        p, den = _softmax_parts(_nt(q3, cmpkv[:, ksl]), cmpf3, scale)
        pc = p / jnp.maximum(den, 1e-30)
        o_cmp = jnp.dot(pc.astype(BF16), cmpkv[:, vsl], preferred_element_type=F32)
        imp = pc[0:tq] + pc[tq:2 * tq] + pc[2 * tq:3 * tq]
        sel = _top_blocks(_block_importance(imp, cur, nb), min(SEL_BLOCKS, nb), nb)
        slcf3 = _rep3(jnp.where(causal, jnp.dot(sel.astype(BF16), expand, preferred_element_type=F32), 0.0))
        p, den = _softmax_parts(_nt(q3, slc_ref[0, :, ksl]), slcf3, scale)
        o_slc = jnp.dot(p.astype(BF16), slc_ref[0, :, vsl], preferred_element_type=F32) / jnp.maximum(den, 1e-30)
        s = jnp.concatenate([_nt(q3, r[0, :, ksl]) for r in win_refs], axis=1)
        p, den = _softmax_parts(s, winf3, scale)
        ph = p.astype(BF16)
        o_win, off = 0.0, 0
        for r in win_refs:
            o_win = o_win + jnp.dot(ph[:, off:off + r.shape[1]], r[0, :, vsl], preferred_element_type=F32)
            off += r.shape[1]
        o_win = o_win / jnp.maximum(den, 1e-30)
        for n in range(HPG_C):
            h = g * HPG_C + n
            sl = slice(h * LANES, (h + 1) * LANES)
            rs = slice(n * tq, (n + 1) * tq)
            o = (o_cmp[rs] * gates[:, 3 * h:3 * h + 1] + o_slc[rs] * gates[:, 3 * h + 1:3 * h + 2]
                 + o_win[rs] * gates[:, 3 * h + 2:3 * h + 3])
            o_ref[:, sl] = (o * _silu(z_ref[:, sl])).astype(o_ref.dtype)


def _nsa_prompt_call(u, post, cmpkv, b, t):
    outs = []
    win_h = post["win_h"].reshape(b, t, 512)
    for band, (q0, nqb, klen) in enumerate(_bands(t)):
        rows = functools.partial(_band_rows, b, t, q0)
        ospec, oshape = _band_out(b, nqb, 768)
        span = nqb * PROMPT_TQ
        if span >= WINDOW and q0 * PROMPT_TQ == band * span:
            blocks = [band - 1, band] if band > 0 else [band]
            win_specs = [pl.BlockSpec((1, span, 512), functools.partial(lambda bi, i, k: (bi, k, 0), k=k))
                         for k in blocks]
            win_starts = tuple(k * span for k in blocks)
        else:
            win_specs = [pl.BlockSpec((1, klen, 512), lambda bi, i: (bi, 0, 0))]
            win_starts = (0,)
        outs.append(pl.pallas_call(
            functools.partial(_nsa_prompt_kernel, q0=q0, win_starts=win_starts),
            grid=(b, nqb),
            in_specs=[
                rows(768), rows(768, _OFF["c_z"] // 768), rows(128, _OFF["c_g"] // 128),
                pl.BlockSpec((1, klen // CMP_BLOCK, 512), lambda bi, i: (bi, 0, 0)),
                pl.BlockSpec((1, klen, 512), lambda bi, i: (bi, 0, 0)),
            ] + win_specs,
            out_specs=ospec, out_shape=oshape,
            compiler_params=_params(("parallel", "arbitrary")),
            name="nsa_prompt",
        )(post["qc"], u, u, cmpkv, post["slc_h"].reshape(b, t, 512), *([win_h] * len(win_specs))))
    return jnp.concatenate(outs, axis=1).reshape(b * t, 768)


def _page_specs(n_pages, rows, layer):
    return [pl.BlockSpec((None, None, rows, LANES), functools.partial(_page_index, layer=layer, page=p))
            for p in range(n_pages)]


def _page_index(b, pt_ref, *, layer, page):
    return (layer, pt_ref[b, page], 0, 0)


def _seq_spec(*shape):
    nd = len(shape)
    return pl.BlockSpec((1,) + shape, lambda b, pt: (b,) + (0,) * nd)


def _dsa_score_kernel(pt_ref, qi_ref, wi_ref, kin_ref, *rest):
    del pt_ref
    pages, o_ref = rest[:-1], rest[-1]
    qi = qi_ref[0]
    w = wi_ref[0] * (IDX_HEADS ** -0.5)
    qh = qi.astype(BF16)
    for p, page in enumerate(pages):
        s = jnp.dot(qh, page[...].astype(BF16), preferred_element_type=F32)
        rel = jnp.maximum(s * (IDX_DIM ** -0.5), 0.0) * w
        o_ref[0, :, p * PAGE_SIZE:(p + 1) * PAGE_SIZE] = jnp.sum(rel, axis=0, keepdims=True)
    s_new = jnp.sum(qi * kin_ref[0], axis=-1, keepdims=True)
    sc = jnp.sum(jnp.maximum(s_new * (IDX_DIM ** -0.5), 0.0) * w, axis=0, keepdims=True)
    lane = lax.broadcasted_iota(I32, (1, LANES), 1)
    n_past = len(pages) * PAGE_SIZE
    o_ref[0, :, n_past:n_past + LANES] = jnp.where(lane == 0, sc, -jnp.inf)


def _dsa_score_call(qi, wi, ki_new, cache_kidx_t, page_table, layer):
    bs, n_pages = page_table.shape
    lk = n_pages * PAGE_SIZE + LANES
    grid_spec = pltpu.PrefetchScalarGridSpec(
        num_scalar_prefetch=1, grid=(bs,),
        in_specs=[_seq_spec(IDX_HEADS, IDX_DIM), _seq_spec(IDX_HEADS, 1), _seq_spec(1, IDX_DIM)]
        + _page_specs(n_pages, IDX_DIM, layer),
        out_specs=_seq_spec(1, lk))
    return pl.pallas_call(
        _dsa_score_kernel, grid_spec=grid_spec,
        out_shape=jax.ShapeDtypeStruct((bs, 1, lk), F32),
        compiler_params=_params(("arbitrary",)),
        name="dsa_score",
    )(page_table, qi.reshape(bs, IDX_HEADS, IDX_DIM), wi.reshape(bs, IDX_HEADS, 1),
      ki_new.reshape(bs, 1, IDX_DIM), *([cache_kidx_t] * n_pages))


def _select_kernel(s_ref, o_ref, key_ref, keyt_ref, *, n_valid, n_keys):
    r, l = s_ref.shape
    kpos = lax.broadcasted_iota(I32, (1, l), 1)
    _topk_mask(s_ref[...], jnp.broadcast_to(kpos < n_valid, (r, l)), n_keys, key_ref, keyt_ref, o_ref)


def _select_call(scores, n_valid, n_keys):
    r, l = scores.shape
    return pl.pallas_call(
        functools.partial(_select_kernel, n_valid=n_valid, n_keys=n_keys),
        grid=(1,),
        in_specs=[pl.BlockSpec((r, l), lambda i: (0, 0))],
        out_specs=pl.BlockSpec((r, l), lambda i: (0, 0)),
        out_shape=jax.ShapeDtypeStruct((r, l), F32),
        scratch_shapes=[pltpu.VMEM((r, l), I32), pltpu.VMEM((l, r), I32)],
        compiler_params=_params(("arbitrary",)),
        name="dsa_select",
    )(scores)


def _new_rows(new_ref, slab_col, first, n):
    out = 0.0
    for k in range(n):
        out = out + jnp.where(slab_col == k, new_ref[first + k:first + k + 1, :], 0.0)
    return out


def _decode_attend(q, slab_col, tokf, new_f, pages, new_ref, n_slabs, scale):
    qh = q.astype(BF16)
    s = jnp.concatenate([_nt(qh, pg[...].astype(BF16)) for pg in pages], axis=1)
    col_slab = lax.broadcasted_iota(I32, (1, s.shape[1]), 1) & (n_slabs - 1)
    maskf = jnp.where(col_slab == slab_col, tokf, 0.0)
    s_new = jnp.sum(q * _new_rows(new_ref, slab_col, 0, n_slabs // 2), axis=-1, keepdims=True)
    sm = jnp.where(maskf > 0.0, s, NEG)
    sn = jnp.where(new_f > 0.0, s_new, NEG)
    m = jnp.maximum(jnp.max(sm, axis=-1, keepdims=True), sn)
    c = scale * LOG2E
    p = jnp.exp2((sm - m) * c) * maskf
    p_new = jnp.exp2((sn - m) * c) * new_f
    den = jnp.sum(p, axis=-1, keepdims=True) + p_new
    return p, p_new, den


def _decode_pv(p, p_new, slab_col, pages, new_ref, n_slabs):
    half = n_slabs // 2
    o = p_new * _new_rows(new_ref, slab_col, half, half)
    off = 0
    for pg in pages:
        rows = pg.shape[0]
        pv = jnp.concatenate([pltpu.roll(p[:, off + c * LANES:off + (c + 1) * LANES], half, 1)
                              for c in range(rows // LANES)], axis=1)
        o = o + jnp.dot(pv.astype(BF16), pg[...].astype(BF16), preferred_element_type=F32)
        off += rows
    return o


def _group_of_row(n_rows, heads_per_group):
    row = lax.broadcasted_iota(I32, (n_rows, 1), 0)
    return jnp.where(row >= heads_per_group, 1, 0)


def _dsa_attn_kernel(pt_ref, q_ref, z_ref, m_ref, mnew_ref, new_ref, *rest):
    del pt_ref
    pages, o_ref = rest[:-1], rest[-1]
    new = new_ref.at[0]
    slab_col = _group_of_row(H_A, HPG_A)
    q = q_ref[0]
    p, p_new, den = _decode_attend(q, slab_col, m_ref[0], mnew_ref[0], pages, new, SLABS_A, HEAD_DIM ** -0.5)
    o = _decode_pv(p, p_new, slab_col, pages, new, SLABS_A) / jnp.maximum(den, 1e-30)
    o_ref[0] = (o * _silu(z_ref[0])).astype(o_ref.dtype)


def _dsa_attn_call(q, z, maskf, kv_new, cache_kv, page_table, layer):
    bs, n_pages = page_table.shape
    n_past = n_pages * PAGE_SIZE
    n_ext = n_past * SLABS_A
    mask_rows = jnp.repeat(maskf[:, :n_past], SLABS_A, axis=1).reshape(bs, 1, n_ext)
    mask_new = maskf[:, n_past:n_past + 1].reshape(bs, 1, 1)
    grid_spec = pltpu.PrefetchScalarGridSpec(
        num_scalar_prefetch=1, grid=(bs,),
        in_specs=[_seq_spec(H_A, HEAD_DIM), _seq_spec(H_A, HEAD_DIM), _seq_spec(1, n_ext), _seq_spec(1, 1),
                  _seq_spec(SLABS_A, LANES)]
        + _page_specs(n_pages, PAGE_SIZE * SLABS_A, layer),
        out_specs=_seq_spec(H_A, HEAD_DIM))
    return pl.pallas_call(
        _dsa_attn_kernel, grid_spec=grid_spec,
        out_shape=jax.ShapeDtypeStruct((bs, H_A, HEAD_DIM), BF16),
        compiler_params=_params(("arbitrary",)),
        name="dsa_attn",
    )(page_table, q.reshape(bs, H_A, HEAD_DIM), z.reshape(bs, H_A, HEAD_DIM), mask_rows, mask_new,
      kv_new.reshape(bs, SLABS_A, LANES), *([cache_kv] * n_pages)).reshape(bs, H_A * HEAD_DIM)


def _diff_dec_kernel(pt_ref, q_ref, z_ref, new_ref, lam_ref, g_ref, *rest, layer):
    del pt_ref
    pages, o_ref = rest[:-1], rest[-1]
    new = new_ref.at[0]
    lam, lam_init = _lam_value(lam_ref, layer)
    row = lax.broadcasted_iota(I32, (2 * H_B, 1), 0)
    lo = lax.broadcasted_iota(I32, (1, LANES), 1) < HALF_DIM
    q2 = jnp.where((row < H_B) == lo, q_ref[0], 0.0)
    slab_col = row & (H_B - 1)
    one = jnp.ones((1, 1), F32)
    p, p_new, den = _decode_attend(q2, slab_col, one, one, pages, new, SLABS_B, HALF_DIM ** -0.5)
    inv = 1.0 / jnp.maximum(den, 1e-30)
    p, p_new = p * inv, p_new * inv
    a = p[0:H_B] - lam * p[H_B:]
    a_new = p_new[0:H_B] - lam * p_new[H_B:]
    o = _subln(_decode_pv(a, a_new, slab_col[0:H_B], pages, new, SLABS_B), g_ref[...], lam_init)
    o_ref[0] = (o * _silu(z_ref[0])).astype(o_ref.dtype)


def _diff_dec_call(q, z, kv_new, cache_kv, page_table, lam_l, subln_l, layer):
    bs, n_pages = page_table.shape
    grid_spec = pltpu.PrefetchScalarGridSpec(
        num_scalar_prefetch=1, grid=(bs,),
        in_specs=[_seq_spec(2 * H_B, HEAD_DIM), _seq_spec(H_B, HEAD_DIM), _seq_spec(SLABS_B, LANES),
                  pl.BlockSpec((4, HALF_DIM), lambda b, pt: (0, 0)),
                  pl.BlockSpec((1, HEAD_DIM), lambda b, pt: (0, 0))]
        + _page_specs(n_pages, PAGE_SIZE * SLABS_B, layer),
        out_specs=_seq_spec(H_B, HEAD_DIM))
    qh = q.reshape(bs, H_B, HEAD_DIM)
    return pl.pallas_call(
        functools.partial(_diff_dec_kernel, layer=layer), grid_spec=grid_spec,
        out_shape=jax.ShapeDtypeStruct((bs, H_B, HEAD_DIM), BF16),
        compiler_params=_params(("arbitrary",)),
        name="diff_dec",
    )(page_table, jnp.concatenate([qh, qh], axis=1), z.reshape(bs, H_B, HEAD_DIM),
      kv_new.reshape(bs, SLABS_B, LANES), lam_l, subln_l.reshape(1, HEAD_DIM),
      *([cache_kv] * n_pages)).reshape(bs, H_B * HEAD_DIM)


def _nsa_dec_kernel(*refs, n_pages, has_prev):
    (pt_ref, q_ref, z_ref, gt_ref, slcn_ref, winn_ref, a_ref, pe_ref, win_ref) = refs[:9]
    del pt_ref
    cmp_pages = refs[9:9 + n_pages]
    slc_pages = refs[9 + n_pages:9 + 2 * n_pages]
    o_ref, wout_ref, cmp_scr = refs[-3], refs[-2], refs[-1]
    del has_prev
    n_past = n_pages * PAGE_SIZE
    nb = n_past // CMP_BLOCK
    per_page = PAGE_SIZE // CMP_BLOCK
    scale = HEAD_DIM ** -0.5
    gates = jax.nn.sigmoid(gt_ref[0])
    slc_new = slcn_ref.at[0]
    win_new = winn_ref.at[0]
    wrows = win_ref.shape[0]
    wout_ref[pl.ds(0, wrows - SLABS_C), :] = win_ref[pl.ds(SLABS_C, wrows - SLABS_C), :]
    wout_ref[pl.ds(wrows - SLABS_C, SLABS_C), :] = win_new[...]
    a = a_ref[...]
    pe_term = jnp.sum(pe_ref[...] * a, axis=0, keepdims=True)
    cmp_scr[nb:, :] = jnp.zeros((LANES - nb, 512), F32)
    for p, page in enumerate(cmp_pages):
        for j in range(per_page):
            bj = p * per_page + j
            for c in range(SLABS_C):
                cmp_scr[bj:bj + 1, c * LANES:(c + 1) * LANES] = _compress_rows(
                    page, j * CMP_BLOCK * SLABS_C, a, pe_term, c)
    cmph = cmp_scr[...].astype(BF16)
    cur = n_past // CMP_BLOCK
    nblk = cur + 1
    lane = lax.broadcasted_iota(I32, (1, LANES), 1)
    cmpf = jnp.where(lane < nb, 1.0, 0.0)
    one = jnp.ones((1, 1), F32)
    q = q_ref[0]
    slab_col = _group_of_row(H_C, HPG_C)
    o_cmp, sels = [], []
    for g in range(KVH_C):
        ksl = slice(g * LANES, (g + 1) * LANES)
        vsl = slice((KVH_C + g) * LANES, (KVH_C + g + 1) * LANES)
        qg = q_ref[0, g * HPG_C:(g + 1) * HPG_C, :]
        p, den = _softmax_parts(_nt(qg.astype(BF16), cmph[:, ksl]), cmpf, scale)
        pc = p / jnp.maximum(den, 1e-30)
        o_cmp.append(jnp.dot(pc.astype(BF16), cmph[:, vsl], preferred_element_type=F32))
        imp = jnp.sum(pc, axis=0, keepdims=True)
        sels.append(_top_blocks(_block_importance(imp, cur, nblk), min(SEL_BLOCKS, nblk), nblk))
    sel = jnp.where(slab_col == 0, sels[0], sels[1])
    prow = lax.broadcasted_iota(I32, (1, PAGE_SIZE * SLABS_C), 1) // (CMP_BLOCK * SLABS_C)
    tok = []
    for p in range(n_pages):
        m = jnp.zeros((H_C, PAGE_SIZE * SLABS_C), F32)
        for j in range(per_page):
            bj = p * per_page + j
            m = jnp.where(prow == j, sel[:, bj:bj + 1], m)
        tok.append(m)
    slcf = jnp.concatenate(tok, axis=1)
    p, p_new, den = _decode_attend(q, slab_col, slcf, sel[:, cur:cur + 1], slc_pages, slc_new, SLABS_C, scale)
    o_slc = _decode_pv(p, p_new, slab_col, slc_pages, slc_new, SLABS_C) / jnp.maximum(den, 1e-30)
    wtok = wrows // SLABS_C
    wrow = lax.broadcasted_iota(I32, (1, wrows), 1)
    winf = jnp.where(wrow >= (wtok + 1 - WINDOW) * SLABS_C, 1.0, 0.0)
    p, p_new, den = _decode_attend(q, slab_col, winf, one, [win_ref], win_new, SLABS_C, scale)
    o_win = _decode_pv(p, p_new, slab_col, [win_ref], win_new, SLABS_C) / jnp.maximum(den, 1e-30)
    for h in range(H_C):
        g, n = divmod(h, HPG_C)
        o = (o_cmp[g][n:n + 1] * gates[:, 3 * h:3 * h + 1] + o_slc[h:h + 1] * gates[:, 3 * h + 1:3 * h + 2]
             + o_win[h:h + 1] * gates[:, 3 * h + 2:3 * h + 3])
        o_ref[0, h:h + 1, :] = (o * _silu(z_ref[0, h:h + 1, :])).astype(o_ref.dtype)


def _nsa_dec_call(q, z, gt, slc_new, win_new, alpha_e, pe_e, win_state, cache_cmp, cache_slc, page_table, layer,
                  win_prev):
    bs, n_pages = page_table.shape
    depth, _, wrows, _ = win_state.shape
    in_specs = [_seq_spec(H_C, HEAD_DIM), _seq_spec(H_C, HEAD_DIM), _seq_spec(1, LANES),
                _seq_spec(SLABS_C, LANES), _seq_spec(SLABS_C, LANES),
                pl.BlockSpec((CMP_BLOCK, 512), lambda b, pt: (0, 0)),
                pl.BlockSpec((CMP_BLOCK, 512), lambda b, pt: (0, 0)),
                pl.BlockSpec((None, None, wrows, LANES), lambda b, pt: (layer, b, 0, 0))]
    in_specs += _page_specs(n_pages, PAGE_SIZE * SLABS_C, layer) + _page_specs(n_pages, PAGE_SIZE * SLABS_C, layer)
    args = [page_table, q.reshape(bs, H_C, HEAD_DIM), z.reshape(bs, H_C, HEAD_DIM), gt.reshape(bs, 1, LANES),
            slc_new.reshape(bs, SLABS_C, LANES), win_new.reshape(bs, SLABS_C, LANES), alpha_e, pe_e, win_state]
    args += [cache_cmp] * n_pages + [cache_slc] * n_pages
    aliases = {}
    if win_prev is not None:
        aliases[len(args)] = 1
        args.append(win_prev)
        in_specs.append(pl.BlockSpec(memory_space=pl.ANY))
    grid_spec = pltpu.PrefetchScalarGridSpec(
        num_scalar_prefetch=1, grid=(bs,),
        in_specs=in_specs,
        out_specs=[_seq_spec(H_C, HEAD_DIM),
                   pl.BlockSpec((None, None, wrows, LANES), lambda b, pt: (layer, b, 0, 0))],
        scratch_shapes=[pltpu.VMEM((LANES, 512), F32)])

    def body(*refs):
        if win_prev is not None:
            refs = refs[:9 + 2 * n_pages] + refs[10 + 2 * n_pages:]
        _nsa_dec_kernel(*refs, n_pages=n_pages, has_prev=win_prev is not None)

    y, win_out = pl.pallas_call(
        body, grid_spec=grid_spec,
        out_shape=[jax.ShapeDtypeStruct((bs, H_C, HEAD_DIM), BF16),
                   jax.ShapeDtypeStruct((depth, bs, wrows, LANES), F32)],
        input_output_aliases=aliases,
        compiler_params=_params(("arbitrary",)),
        name="nsa_dec",
    )(*args)
    return y.reshape(bs, H_C * HEAD_DIM), win_out


def _outproj_kernel(ma_ref, mb_ref, mc_ref, w_ref, x_ref, gate_ref, fg_ref, o_ref, *, final):
    wa, wb = H_A * HEAD_DIM, (H_A + H_B) * HEAD_DIM
    out = jnp.dot(ma_ref[...], w_ref[0:wa, :], preferred_element_type=F32)
    out = out + jnp.dot(mb_ref[...], w_ref[wa:wb, :], preferred_element_type=F32)
    out = out + jnp.dot(mc_ref[...], w_ref[wb:, :], preferred_element_type=F32)
    xn = x_ref[...] + gate_ref[...] * out
    if final:
        xn = xn * lax.rsqrt(jnp.mean(xn * xn, axis=-1, keepdims=True) + NORM_EPS) * fg_ref[...]
    o_ref[...] = xn


def _outproj_call(ma, mb, mc, w, x, gate, fg, rows_per_batch, final):
    m, d = x.shape
    tm = min(m, 256)
    if rows_per_batch > 1:
        per = rows_per_batch // tm
        gspec = pl.BlockSpec((None, 1, d), lambda i: (i // per, 0, 0))
    else:
        gspec = pl.BlockSpec((tm, d), lambda i: (i, 0))

    def rows(width):
        return pl.BlockSpec((tm, width), lambda i: (i, 0))

    return pl.pallas_call(
        functools.partial(_outproj_kernel, final=final),
        grid=(m // tm,),
        in_specs=[rows(768), rows(512), rows(768), pl.BlockSpec((d, d), lambda i: (0, 0)), rows(d), gspec,
                  pl.BlockSpec((1, d), lambda i: (0, 0))],
        out_specs=rows(d),
        out_shape=jax.ShapeDtypeStruct((m, d), F32),
        compiler_params=_params(("parallel",)),
        name="outproj",
    )(ma, mb, mc, w, x, gate, fg)


_SRC = {}
_o = 0
for _n, _w in _IN_SPLITS:
    _SRC[_n] = (_o, _w)
    _o += _w
IN_WIDTH = _o


def _pack_kernel(w_ref, o_ref):
    rows = w_ref.shape[0]
    for name, width in _PAD_ORDER:
        dst = _OFF[name]
        sw = 0
        if name in _SRC:
            s, sw = _SRC[name]
            o_ref[:, dst:dst + sw] = w_ref[:, s:s + sw].astype(BF16)
        if width > sw:
            o_ref[:, dst + sw:dst + width] = jnp.zeros((rows, width - sw), BF16)


def _pad_w_in(w_in):
    depth, d, width = w_in.shape
    assert width == IN_WIDTH
    tr = 256
    return pl.pallas_call(
        _pack_kernel,
        grid=(depth, d // tr),
        in_specs=[pl.BlockSpec((None, tr, width), lambda l, i: (l, i, 0))],
        out_specs=pl.BlockSpec((None, tr, NP), lambda l, i: (l, i, 0)),
        out_shape=jax.ShapeDtypeStruct((depth, d, NP), BF16),
        compiler_params=_params(("parallel", "parallel")),
        name="pack_w_in",
    )(w_in)


def _ucol(u, name, width=None):
    width = width or dict(_PAD_ORDER)[name]
    return u[:, _OFF[name]:_OFF[name] + width]


def kernel(x_prompt, x_sample, cache_a_kv, cache_a_kidx, cache_b_kv, cache_c_cmp_kv, cache_c_slc_kv,
           state_c_win_kv, page_table, c_prompt, c_sample, w_mod, b_mod, norm_g, w_in, w_out, lam,
           subln_g, cmp_alpha, cmp_pe, final_g):
    depth, d, _ = w_mod.shape
    b, t, _ = x_prompt.shape
    bs, ts, _ = x_sample.shape
    assert ts == 1 and t % 256 == 0
    n_pool = cache_a_kv.shape[1]
    n_pages = page_table.shape[1]
    n_past = n_pages * PAGE_SIZE
    wbuf = state_c_win_kv.shape[2]
    assert wbuf == WINDOW and n_past >= WINDOW

    w_in_p = _pad_w_in(w_in)
    w_out_h = w_out.astype(BF16)
    mod = _mod_call(jnp.concatenate([c_prompt, c_sample], axis=0), w_mod, b_mod)
    alpha_e = jnp.broadcast_to(cmp_alpha[..., None], cmp_alpha.shape + (HEAD_DIM,)).reshape(depth, CMP_BLOCK, 512)
    pe_e = cmp_pe.reshape(depth, CMP_BLOCK, 512)
    fg = final_g.reshape(1, d)

    ca_kv = cache_a_kv.reshape(depth, n_pool, PAGE_SIZE * SLABS_A, LANES)
    cb_kv = cache_b_kv.reshape(depth, n_pool, PAGE_SIZE * SLABS_B, LANES)
    cc_cmp = cache_c_cmp_kv.reshape(depth, n_pool, PAGE_SIZE * SLABS_C, LANES)
    cc_slc = cache_c_slc_kv.reshape(depth, n_pool, PAGE_SIZE * SLABS_C, LANES)
    win_state = state_c_win_kv.reshape(depth, bs, wbuf * SLABS_C, LANES)
    ca_kidx_t = jnp.swapaxes(cache_a_kidx, 2, 3)

    tabs_p = _rope_tables(jnp.arange(t, dtype=I32))
    tabs_s = _rope_tables(n_past + jnp.arange(1, dtype=I32))

    x = x_prompt.reshape(b * t, d)
    post = None
    for layer in range(depth):
        shift, scale, gate = (mod[layer, :b, i * d:(i + 1) * d].reshape(b, 1, d) for i in range(3))
        u = _inproj_call(x, norm_g[layer].reshape(1, d), scale, shift, w_in_p, layer, t)
        post = _post_call(u, tabs_p, t, layer, depth, post)
        ya = _dsa_prompt_call(u, post, b, t)
        yb = _diff_prompt_call(u, post, lam[layer], subln_g[layer], layer, b, t)
        cmpkv = _compress_call(post["cmp"], alpha_e[layer], pe_e[layer], layer, b, t)
        yc = _nsa_prompt_call(u, post, cmpkv, b, t)
        x = _outproj_call(ya, yb, yc, w_out_h[layer], x, gate, fg, t, layer == depth - 1)
    y_prompt = x.reshape(b, t, d)
    st_p = post

    x = x_sample.reshape(bs, d)
    post = None
    win_out = None
    for layer in range(depth):
        shift, scale, gate = (mod[layer, b:, i * d:(i + 1) * d] for i in range(3))
        u = _inproj_call(x, norm_g[layer].reshape(1, d), scale, shift, w_in_p, layer, 1)
        post = _post_call(u, tabs_s, 1, layer, depth, post)
        n_keys = min(TOPK_MAX, (n_past + 1) // 4)
        scores = _dsa_score_call(post["qi"], _ucol(u, "a_wi", IDX_HEADS), post["a_kidx"][layer], ca_kidx_t,
                                 page_table, layer)
        maskf = _select_call(scores.reshape(bs, -1), n_past + 1, n_keys)
        ya = _dsa_attn_call(post["qa"], _ucol(u, "a_z"), maskf, post["a_kv"][layer], ca_kv, page_table, layer)
        yb = _diff_dec_call(post["qb"], _ucol(u, "b_z"), post["b_kv"][layer], cb_kv, page_table, lam[layer],
                            subln_g[layer], layer)
        yc, win_out = _nsa_dec_call(post["qc"], _ucol(u, "c_z"), _ucol(u, "c_g"), post["slc"][layer],
                                    post["win"][layer], alpha_e[layer], pe_e[layer], win_state, cc_cmp, cc_slc,
                                    page_table, layer, win_out)
        x = _outproj_call(ya, yb, yc, w_out_h[layer], x, gate, fg, 1, layer == depth - 1)
    y_sample = x.reshape(bs, 1, d)
    st_s = post

    outs = []
    for name, tail in (("a_kv", (2, KVH_A, HEAD_DIM)), ("a_kidx", (IDX_DIM,)), ("b_kv", (2, H_B, HEAD_DIM)),
                       ("cmp", (2, KVH_C, HEAD_DIM)), ("slc", (2, KVH_C, HEAD_DIM))):
        outs.append(st_p[name].reshape((depth, b, t) + tail))
        outs.append(st_s[name].reshape((depth, bs, 1) + tail))
    keep = min(WINDOW, t)
    win_p = st_p["win"].reshape(depth, b, t, 2, KVH_C, HEAD_DIM)[:, :, t - keep:]
    win_s = win_out.reshape(depth, bs, wbuf, 2, KVH_C, HEAD_DIM)
    return (y_prompt, y_sample, *outs, win_p, win_s)
```

```python
import functools
import math

import jax
import jax.numpy as jnp
from jax import lax
from jax.experimental import pallas as pl
from jax.experimental.pallas import tpu as pltpu

F32 = jnp.float32
BF16 = jnp.bfloat16
I32 = jnp.int32

HEAD_DIM = 128
H_A, H_B, H_C = 6, 4, 6
KVH_A, KVH_C = 2, 2
HPG_A, HPG_C = H_A // KVH_A, H_C // KVH_C
IDX_HEADS, IDX_DIM = 16, 64
TOPK_MAX = 256
HALF_DIM = 64
SUBLN_EPS = 1e-5
CMP_BLOCK = 64
SEL_BLOCKS = 8
WINDOW = 512
ROPE_THETA = 10000.0
NORM_EPS = 1e-6
PAGE_SIZE = 128

LANES = 128
VMEM_LIMIT = 56 * 1024 * 1024

NEG = -1e30
INT_MIN = -2147483648
SLABS_A = 2 * KVH_A
SLABS_B = 2 * H_B
SLABS_C = 2 * KVH_C
PROMPT_BANDS = 4

_IN_SPLITS = (
    ("a_q", 768), ("a_k", 256), ("a_v", 256), ("a_qi", 1024), ("a_ki", 64), ("a_wi", 16), ("a_z", 768),
    ("b_q", 512), ("b_k", 512), ("b_v", 512), ("b_z", 512),
    ("c_q", 768), ("c_kc", 256), ("c_vc", 256), ("c_ks", 256), ("c_vs", 256), ("c_kw", 256),
    ("c_vw", 256), ("c_g", 18), ("c_z", 768),
)
_PAD_ORDER = (
    ("a_q", 768), ("a_z", 768), ("c_q", 768), ("c_z", 768), ("a_qi", 1024),
    ("b_k", 512), ("b_v", 512), ("b_q", 512), ("b_z", 512),
    ("a_k", 256), ("a_v", 256), ("c_kc", 256), ("c_vc", 256), ("c_ks", 256), ("c_vs", 256),
    ("c_kw", 256), ("c_vw", 256), ("a_ki", 128), ("a_wi", 128), ("c_g", 128), ("_pad", 128),
)
_OFF = {}
_o = 0
for _n, _w in _PAD_ORDER:
    _OFF[_n] = _o
    _o += _w
NP = _o
IN_TN = 512


def _params(sem):
    return pltpu.CompilerParams(dimension_semantics=sem, vmem_limit_bytes=VMEM_LIMIT)


def _nt(a, b):
    return lax.dot_general(a, b, (((1,), (1,)), ((), ())), preferred_element_type=F32)


def _silu(z):
    return z * jax.nn.sigmoid(z)


def _slab(ref, c, n_slabs):
    return ref[pl.ds(c, ref.shape[0] // n_slabs, stride=n_slabs), :]


def _mod_kernel(c_ref, w_ref, b_ref, o_ref):
    c = c_ref[...]
    o_ref[0] = jnp.dot(_silu(c).astype(BF16), w_ref[0].astype(BF16), preferred_element_type=F32) + b_ref[0]


def _mod_call(c_all, w_mod, b_mod):
    depth, d, n3 = w_mod.shape
    bc = c_all.shape[0]
    tn = 768
    return pl.pallas_call(
        _mod_kernel,
        grid=(depth, n3 // tn),
        in_specs=[
            pl.BlockSpec((bc, d), lambda l, j: (0, 0)),
            pl.BlockSpec((1, d, tn), lambda l, j: (l, 0, j)),
            pl.BlockSpec((1, 1, tn), lambda l, j: (l, 0, j)),
        ],
        out_specs=pl.BlockSpec((1, bc, tn), lambda l, j: (l, 0, j)),
        out_shape=jax.ShapeDtypeStruct((depth, bc, n3), F32),
        compiler_params=_params(("parallel", "parallel")),
        name="mod",
    )(c_all, w_mod, b_mod.reshape(depth, 1, n3))


def _inproj_kernel(x_ref, g_ref, sc_ref, sh_ref, w_ref, o_ref, h_ref):
    @pl.when(pl.program_id(1) == 0)
    def _():
        x = x_ref[...]
        y = x * lax.rsqrt(jnp.mean(x * x, axis=-1, keepdims=True) + NORM_EPS) * g_ref[...]
        h_ref[...] = (y * (1.0 + sc_ref[...]) + sh_ref[...]).astype(BF16)

    o_ref[...] = jnp.dot(h_ref[...], w_ref[...], preferred_element_type=F32)


def _inproj_call(x, g, scale, shift, w, layer, rows_per_batch):
    m, d = x.shape
    tm = min(m, 1024)
    if rows_per_batch > 1:
        per = rows_per_batch // tm
        mspec = pl.BlockSpec((None, 1, d), lambda i, j: (i // per, 0, 0))
    else:
        mspec = pl.BlockSpec((tm, d), lambda i, j: (i, 0))
    return pl.pallas_call(
        _inproj_kernel,
        grid=(m // tm, NP // IN_TN),
        in_specs=[
            pl.BlockSpec((tm, d), lambda i, j: (i, 0)),
            pl.BlockSpec((1, d), lambda i, j: (0, 0)),
            mspec, mspec,
            pl.BlockSpec((None, d, IN_TN), lambda i, j: (layer, 0, j)),
        ],
        out_specs=pl.BlockSpec((tm, IN_TN), lambda i, j: (i, j)),
        out_shape=jax.ShapeDtypeStruct((m, NP), F32),
        scratch_shapes=[pltpu.VMEM((tm, d), BF16)],
        compiler_params=_params(("parallel", "arbitrary")),
        name="inproj",
    )(x, g, scale, shift, w)


def _rope128(x, cos, sin_s):
    return x * cos + pltpu.roll(x, 64, 1) * sin_s


def _rope64(x, cos, sin_s, low32):
    partner = jnp.where(low32, pltpu.roll(x, 96, 1), pltpu.roll(x, 32, 1))
    return x * cos + partner * sin_s


_POST_STATE = ("a_kv", "b_kv", "cmp", "slc", "win", "a_kidx")


def _post_kernel(*refs, n_alias):
    (c128_ref, s128_ref, c64_ref, s64_ref,
     uqa_ref, uqc_ref, uqi_ref, ubkv_ref, ubq_ref, uakv_ref, ucmp_ref, uslc_ref, uwin_ref, uki_ref) = refs[:14]
    (qa_ref, qc_ref, qi_ref, qb_ref, akv_ref, bkv_ref, cmp_ref, slc_ref, win_ref, ki_ref,
     bkvh_ref, akvh_ref, slch_ref, winh_ref, kid_ref) = refs[14 + n_alias:]
    c128, s128 = c128_ref[...], s128_ref[...]
    c64, s64 = c64_ref[...], s64_ref[...]
    lane = lax.broadcasted_iota(I32, (1, LANES), 1)
    low32 = (lane % 64) < 32
    tq = uqa_ref.shape[0]

    def rot128(x):
        return _rope128(x, c128, s128)

    def rot64(x):
        return _rope64(x, c64, s64, low32)

    def queries(src, dst, n, rot):
        for h in range(n):
            sl = slice(h * LANES, (h + 1) * LANES)
            dst[:, sl] = rot(src[:, sl])

    def cache_rows(src, dst, dsth, n_slabs, rot):
        for c in range(n_slabs):
            sl = slice(c * LANES, (c + 1) * LANES)
            val = rot(src[:, sl]) if c < n_slabs // 2 else src[:, sl]
            dst[pl.ds(c, tq, stride=n_slabs), :] = val
            if dsth is not None:
                dsth[:, sl] = val.astype(BF16)

    queries(uqa_ref, qa_ref, H_A, rot128)
    queries(uqc_ref, qc_ref, H_C, rot128)
    queries(uqi_ref, qi_ref, IDX_HEADS // 2, rot64)
    queries(ubq_ref, qb_ref, H_B, rot64)
    cache_rows(ubkv_ref, bkv_ref, bkvh_ref, SLABS_B, rot64)
    cache_rows(uakv_ref, akv_ref, akvh_ref, SLABS_A, rot128)
    cache_rows(ucmp_ref, cmp_ref, None, SLABS_C, rot128)
    cache_rows(uslc_ref, slc_ref, slch_ref, SLABS_C, rot128)
    cache_rows(uwin_ref, win_ref, winh_ref, SLABS_C, rot128)
    ki = rot64(uki_ref[...])
    ki_ref[...] = ki[:, :IDX_DIM]
    kid_ref[...] = (ki + pltpu.roll(ki, 64, 1)).astype(BF16)


def _post_call(u, tabs, rows_per_batch, layer, depth, prev):
    m = u.shape[0]
    tq = min(m, 256)
    if rows_per_batch > 1:
        per = rows_per_batch // tq
        tspec = pl.BlockSpec((tq, LANES), lambda i: (i % per, 0))
    else:
        tspec = pl.BlockSpec((1, LANES), lambda i: (0, 0))

    def col(name, width):
        blk = _OFF[name] // width
        assert blk * width == _OFF[name]
        return pl.BlockSpec((tq, width), lambda i: (i, blk))

    def row(width):
        return pl.BlockSpec((tq, width), lambda i: (i, 0))

    def state(rows_per_token, width=LANES):
        return pl.BlockSpec((None, tq * rows_per_token, width), lambda i: (layer, i, 0))

    in_specs = [tspec] * 4 + [
        col("a_q", 768), col("c_q", 768), col("a_qi", 1024), col("b_k", 1024), col("b_q", 512),
        col("a_k", 512), col("c_kc", 512), col("c_ks", 512), col("c_kw", 512), col("a_ki", 128),
    ]
    args = list(tabs) + [u] * 10
    aliases = {}
    if prev is not None:
        for k, name in enumerate(_POST_STATE):
            aliases[len(args)] = 4 + k
            args.append(prev[name])
            in_specs.append(pl.BlockSpec(memory_space=pl.ANY))
    qouts = [(768, F32), (768, F32), (1024, F32), (512, F32)]
    houts = [(1024, BF16), (512, BF16), (512, BF16), (512, BF16), (128, BF16)]
    souts = [(SLABS_A, LANES), (SLABS_B, LANES), (SLABS_C, LANES), (SLABS_C, LANES), (SLABS_C, LANES), (1, IDX_DIM)]
    res = pl.pallas_call(
        functools.partial(_post_kernel, n_alias=len(aliases)),
        grid=(m // tq,),
        in_specs=in_specs,
        out_specs=[row(w) for w, _ in qouts] + [state(r, w) for r, w in souts] + [row(w) for w, _ in houts],
        out_shape=[jax.ShapeDtypeStruct((m, w), dt) for w, dt in qouts]
        + [jax.ShapeDtypeStruct((depth, m * r, w), F32) for r, w in souts]
        + [jax.ShapeDtypeStruct((m, w), dt) for w, dt in houts],
        input_output_aliases=aliases,
        compiler_params=_params(("parallel",)),
        name="post",
    )(*args)
    names = ("qa", "qc", "qi", "qb") + _POST_STATE + ("b_kv_h", "a_kv_h", "slc_h", "win_h", "kidx_dup")
    return dict(zip(names, res))


def _rope_tables(pos):
    def tab(d):
        inv = ROPE_THETA ** (-jnp.arange(0, d, 2, dtype=F32) / d)
        ang = pos.astype(F32)[:, None] * inv[None, :]
        cos, sin = jnp.cos(ang), jnp.sin(ang)
        reps = LANES // d
        return jnp.tile(jnp.concatenate([cos, cos], -1), (1, reps)), jnp.tile(jnp.concatenate([-sin, sin], -1), (1, reps))

    c128, s128 = tab(HEAD_DIM)
    c64, s64 = tab(IDX_DIM)
    return c128, s128, c64, s64


LOG2E = 1.4426950408889634


def _mask_bias(maskf):
    return jnp.where(maskf > 0.0, 0.0, NEG)


def _softmax_parts(s, bias, scale, maskf=None):
    sm = s + bias
    m = jnp.max(sm, axis=-1, keepdims=True)
    p = jnp.exp2((sm - m) * (scale * LOG2E))
    if maskf is not None:
        p = p * maskf
    return p, jnp.sum(p, axis=-1, keepdims=True)


def _row_to_col(x):
    return jnp.broadcast_to(x, (LANES, x.shape[1])).T[:, 0:1]


def _topk_mask(score, valid, k, key_ref, keyt_ref, mask_ref):
    r, l = score.shape
    nchunk = l // LANES
    bits = lax.bitcast_convert_type(score + 0.0, I32)
    key = bits ^ ((bits >> 31) & 0x7FFFFFFF)
    key_ref[...] = jnp.where(valid, key, INT_MIN)
    for c in range(nchunk):
        sl = slice(c * LANES, (c + 1) * LANES)
        keyt_ref[sl, :] = lax.bitcast_convert_type(lax.bitcast_convert_type(key_ref[:, sl], F32).T, I32)
    kf = float(k)

    def count(cand, strict):
        acc = jnp.zeros((LANES, r), F32)
        for c in range(nchunk):
            kc = keyt_ref[c * LANES:(c + 1) * LANES, :]
            acc = acc + jnp.where((kc > cand) if strict else (kc >= cand), 1.0, 0.0)
        return jnp.sum(acc, axis=0, keepdims=True)

    def body(it, t_u):
        cand_u = t_u | lax.shift_left(jnp.int32(1), 31 - it)
        return jnp.where(count(cand_u ^ INT_MIN, False) >= kf, cand_u, t_u)

    t_u = lax.fori_loop(0, 32, body, jnp.zeros((1, r), I32))
    thr_t = t_u ^ INT_MIN
    need_t = kf - count(thr_t, True)
    tied_t = jnp.where(count(thr_t, False) > kf, jnp.where(thr_t != INT_MIN, 1.0, 0.0), 0.0)
    thr = lax.bitcast_convert_type(_row_to_col(lax.bitcast_convert_type(thr_t, F32)), I32)
    has_tie = jnp.max(tied_t) > 0.0

    @pl.when(jnp.logical_not(has_tie))
    def _():
        for c in range(nchunk):
            sl = slice(c * LANES, (c + 1) * LANES)
            kc = key_ref[:, sl]
            mask_ref[:, sl] = jnp.where(kc >= thr, jnp.where(kc != INT_MIN, 1.0, 0.0), 0.0)

    @pl.when(has_tie)
    def _():
        need = _row_to_col(need_t)
        ri = lax.broadcasted_iota(I32, (LANES, LANES), 0)
        ci = lax.broadcasted_iota(I32, (LANES, LANES), 1)
        tri = jnp.where(ri < ci, 1.0, 0.0).astype(BF16)
        carry = jnp.zeros((r, 1), F32)
        for c in range(nchunk):
            sl = slice(c * LANES, (c + 1) * LANES)
            kc = key_ref[:, sl]
            eq = jnp.where(kc == thr, 1.0, 0.0)
            pre = jnp.dot(eq.astype(BF16), tri, preferred_element_type=F32) + carry
            take = jnp.where(kc > thr, 1.0, jnp.where(pre < need, eq, 0.0))
            mask_ref[:, sl] = jnp.where(kc != INT_MIN, take, 0.0)
            carry = carry + jnp.sum(eq, axis=-1, keepdims=True)


def _top_blocks(imp, nsel, nblk):
    lane = lax.broadcasted_iota(I32, (1, imp.shape[1]), 1)
    rank = jnp.zeros(imp.shape, F32)
    for i in range(nblk):
        col = imp[:, i:i + 1]
        later = jnp.where(lane > i, 1.0, 0.0)
        rank = rank + jnp.where(col > imp, 1.0, jnp.where(col == imp, later, 0.0))
    return jnp.where(rank < float(nsel), jnp.where(imp > -jnp.inf, 1.0, 0.0), 0.0)


def _block_importance(imp, cur, nblk):
    blk = lax.broadcasted_iota(I32, imp.shape, 1)
    forced = (blk == 0) | (blk == cur) | (blk == cur - 1)
    imp = jnp.where(forced, jnp.inf, imp)
    return jnp.where((blk <= cur) & (blk < nblk), imp, -jnp.inf)


def _lam_value(lam_ref, layer):
    lp = lam_ref[...]
    a = jnp.sum(lp[0:1] * lp[1:2], axis=-1, keepdims=True)
    b = jnp.sum(lp[2:3] * lp[3:4], axis=-1, keepdims=True)
    lam_init = 0.8 - 0.6 * math.exp(-0.3 * layer)
    return jnp.exp(a) - jnp.exp(b) + lam_init, lam_init


def _subln(o, g, lam_init):
    y = o * lax.rsqrt(jnp.mean(o * o, axis=-1, keepdims=True) + SUBLN_EPS)
    return y * g * (1.0 - lam_init)


PROMPT_TQ = 128


def _bands(t):
    nq = t // PROMPT_TQ
    nb = PROMPT_BANDS if nq % PROMPT_BANDS == 0 else 1
    per = nq // nb
    return [(i * per, per, (i + 1) * per * PROMPT_TQ) for i in range(nb)]


def _band_rows(b, t, q0, width, blk=0):
    nq = t // PROMPT_TQ
    return pl.BlockSpec((PROMPT_TQ, width), lambda bi, i: (bi * nq + q0 + i, blk))


def _band_out(b, nqb, width):
    return (pl.BlockSpec((None, PROMPT_TQ, width), lambda bi, i: (bi, i, 0)),
            jax.ShapeDtypeStruct((b, nqb * PROMPT_TQ, width), BF16))


def _dsa_prompt_kernel(q_ref, qi_ref, wi_ref, z_ref, kid_ref, kv_ref, o_ref, key_ref, keyt_ref, mask_ref, *,
                       n_keys, q0):
    tq = q_ref.shape[0]
    t = kv_ref.shape[1]
    qpos = (q0 + pl.program_id(1)) * tq + lax.broadcasted_iota(I32, (tq, 1), 0)
    kpos = lax.broadcasted_iota(I32, (1, t), 1)
    lane = lax.broadcasted_iota(I32, (1, LANES), 1)
    lo = lane < IDX_DIM
    kd = kid_ref[0]
    w = wi_ref[...] * (IDX_HEADS ** -0.5 * IDX_DIM ** -0.5)
    acc = jnp.zeros((tq, t), F32)
    for pr in range(IDX_HEADS // 2):
        qp = qi_ref[:, pr * LANES:(pr + 1) * LANES]
        for half in range(2):
            qm = jnp.where(lo if half == 0 else jnp.logical_not(lo), qp, 0.0).astype(BF16)
            h = 2 * pr + half
            acc = acc + w[:, h:h + 1] * jnp.maximum(_nt(qm, kd), 0.0)
    _topk_mask(acc, kpos <= qpos, n_keys, key_ref, keyt_ref, mask_ref)
    bias3 = _rep3(_mask_bias(mask_ref[...]))
    for g in range(KVH_A):
        q3 = jnp.concatenate([q_ref[:, (g * HPG_A + n) * LANES:(g * HPG_A + n + 1) * LANES].astype(BF16)
                              for n in range(HPG_A)], axis=0)
        p, den = _softmax_parts(_nt(q3, kv_ref[0, :, g * LANES:(g + 1) * LANES]), bias3, HEAD_DIM ** -0.5)
        o3 = jnp.dot(p.astype(BF16), kv_ref[0, :, (KVH_A + g) * LANES:(KVH_A + g + 1) * LANES],
                     preferred_element_type=F32) / jnp.maximum(den, 1e-30)
        for n in range(HPG_A):
            sl = slice((g * HPG_A + n) * LANES, (g * HPG_A + n + 1) * LANES)
            o_ref[:, sl] = (o3[n * tq:(n + 1) * tq] * _silu(z_ref[:, sl])).astype(o_ref.dtype)


def _dsa_prompt_call(u, post, b, t):
    n_keys = min(TOPK_MAX, t // 4)
    outs = []
    for q0, nqb, klen in _bands(t):
        rows = functools.partial(_band_rows, b, t, q0)
        ospec, oshape = _band_out(b, nqb, 768)
        outs.append(pl.pallas_call(
            functools.partial(_dsa_prompt_kernel, n_keys=n_keys, q0=q0),
            grid=(b, nqb),
            in_specs=[
                rows(768), rows(1024),
                rows(128, _OFF["a_wi"] // 128), rows(768, _OFF["a_z"] // 768),
                pl.BlockSpec((1, klen, 128), lambda bi, i: (bi, 0, 0)),
                pl.BlockSpec((1, klen, 512), lambda bi, i: (bi, 0, 0)),
            ],
            out_specs=ospec, out_shape=oshape,
            scratch_shapes=[pltpu.VMEM((PROMPT_TQ, klen), I32), pltpu.VMEM((klen, PROMPT_TQ), I32),
                            pltpu.VMEM((PROMPT_TQ, klen), F32)],
            compiler_params=_params(("parallel", "arbitrary")),
            name="dsa_prompt",
        )(post["qa"], post["qi"], u, u, post["kidx_dup"].reshape(b, t, 128), post["a_kv_h"].reshape(b, t, 512)))
    return jnp.concatenate(outs, axis=1).reshape(b * t, 768)


def _diff_prompt_kernel(q_ref, z_ref, kv_ref, lam_ref, g_ref, o_ref, *, layer, q0):
    tq = q_ref.shape[0]
    t = kv_ref.shape[1]
    qpos = (q0 + pl.program_id(1)) * tq + lax.broadcasted_iota(I32, (tq, 1), 0)
    kpos = lax.broadcasted_iota(I32, (1, t), 1)
    bias = jnp.where(kpos <= qpos, 0.0, NEG)
    bias2 = jnp.concatenate([bias, bias], axis=0)
    lane = lax.broadcasted_iota(I32, (1, LANES), 1)
    lo = lane < HALF_DIM
    lam, lam_init = _lam_value(lam_ref, layer)
    for h in range(H_B):
        sl = slice(h * LANES, (h + 1) * LANES)
        q = q_ref[:, sl]
        qm = jnp.concatenate([jnp.where(lo, q, 0.0), jnp.where(lo, 0.0, q)], axis=0).astype(BF16)
        p, den = _softmax_parts(_nt(qm, kv_ref[0, :, sl]), bias2, HALF_DIM ** -0.5)
        pr = p / jnp.maximum(den, 1e-30)
        a = pr[0:tq] - lam * pr[tq:2 * tq]
        o = jnp.dot(a.astype(BF16), kv_ref[0, :, (H_B + h) * LANES:(H_B + h + 1) * LANES],
                    preferred_element_type=F32)
        o = _subln(o, g_ref[...], lam_init)
        o_ref[:, sl] = (o * _silu(z_ref[:, sl])).astype(o_ref.dtype)


def _diff_prompt_call(u, post, lam_l, subln_l, layer, b, t):
    outs = []
    for q0, nqb, klen in _bands(t):
        rows = functools.partial(_band_rows, b, t, q0)
        ospec, oshape = _band_out(b, nqb, 512)
        outs.append(pl.pallas_call(
            functools.partial(_diff_prompt_kernel, layer=layer, q0=q0),
            grid=(b, nqb),
            in_specs=[
                rows(512), rows(512, _OFF["b_z"] // 512),
                pl.BlockSpec((1, klen, 1024), lambda bi, i: (bi, 0, 0)),
                pl.BlockSpec((4, HALF_DIM), lambda bi, i: (0, 0)),
                pl.BlockSpec((1, HEAD_DIM), lambda bi, i: (0, 0)),
            ],
            out_specs=ospec, out_shape=oshape,
            compiler_params=_params(("parallel", "arbitrary")),
            name="diff_prompt",
        )(post["qb"], u, post["b_kv_h"].reshape(b, t, 1024), lam_l, subln_l.reshape(1, HEAD_DIM)))
    return jnp.concatenate(outs, axis=1).reshape(b * t, 512)


def _compress_rows(rows_ref, first_row, a, pe_term, c):
    sl = slice(c * LANES, (c + 1) * LANES)
    blk = rows_ref[pl.ds(first_row + c, CMP_BLOCK, stride=SLABS_C), :]
    return jnp.sum(blk * a[:, sl], axis=0, keepdims=True) + pe_term[:, sl]


def _compress_kernel(rows_ref, a_ref, pe_ref, o_ref):
    nb = o_ref.shape[1]
    a = a_ref[...]
    pe_term = jnp.sum(pe_ref[...] * a, axis=0, keepdims=True)
    for j in range(nb):
        for c in range(SLABS_C):
            o_ref[0, j:j + 1, c * LANES:(c + 1) * LANES] = _compress_rows(
                rows_ref, j * CMP_BLOCK * SLABS_C, a, pe_term, c)


def _compress_call(cmp_rows, alpha_e, pe_e, layer, b, t):
    nb = t // CMP_BLOCK
    return pl.pallas_call(
        _compress_kernel,
        grid=(b,),
        in_specs=[
            pl.BlockSpec((None, t * SLABS_C, LANES), lambda bi: (layer, bi, 0)),
            pl.BlockSpec((CMP_BLOCK, 512), lambda bi: (0, 0)),
            pl.BlockSpec((CMP_BLOCK, 512), lambda bi: (0, 0)),
        ],
        out_specs=pl.BlockSpec((1, nb, 512), lambda bi: (bi, 0, 0)),
        out_shape=jax.ShapeDtypeStruct((b, nb, 512), F32),
        compiler_params=_params(("parallel",)),
        name="compress",
    )(cmp_rows, alpha_e, pe_e)


def _rep3(x):
    return jnp.concatenate([x] * HPG_C, axis=0)


def _nsa_prompt_kernel(q_ref, z_ref, gt_ref, cmp_ref, slc_ref, *rest, q0, win_starts):
    win_refs, o_ref = rest[:-1], rest[-1]
    tq = q_ref.shape[0]
    t = slc_ref.shape[1]
    nb = cmp_ref.shape[1]
    qpos = (q0 + pl.program_id(1)) * tq + lax.broadcasted_iota(I32, (tq, 1), 0)
    kpos = lax.broadcasted_iota(I32, (1, t), 1)
    causal = kpos <= qpos
    wpos = jnp.concatenate([s + lax.broadcasted_iota(I32, (1, r.shape[1]), 1) for r, s in zip(win_refs, win_starts)],
                           axis=1)
    wbias3 = _rep3(jnp.where((wpos <= qpos) & (wpos > qpos - WINDOW), 0.0, NEG))
    blk = lax.broadcasted_iota(I32, (1, nb), 1)
    cmpf3 = _rep3(jnp.where((blk + 1) * CMP_BLOCK - 1 <= qpos, 1.0, 0.0))
    cbias3 = _mask_bias(cmpf3)
    cur = qpos // CMP_BLOCK
    expand = jnp.where(lax.broadcasted_iota(I32, (nb, t), 1) // CMP_BLOCK == lax.broadcasted_iota(I32, (nb, t), 0),
                       1.0, 0.0).astype(BF16)
    gates = jax.nn.sigmoid(gt_ref[...])
    scale = HEAD_DIM ** -0.5
    cmpkv = cmp_ref[0].astype(BF16)
    for g in range(KVH_C):
        ksl = slice(g * LANES, (g + 1) * LANES)
        vsl = slice((KVH_C + g) * LANES, (KVH_C + g + 1) * LANES)
        q3 = jnp.concatenate([q_ref[:, (g * HPG_C + n) * LANES:(g * HPG_C + n + 1) * LANES].astype(BF16)
                              for n in range(HPG_C)], axis=0)
        p, den = _softmax_parts(_nt(q3, cmpkv[:, ksl]), cbias3, scale, cmpf3)
        pc = p / jnp.maximum(den, 1e-30)
        o_cmp = jnp.dot(pc.astype(BF16), cmpkv[:, vsl], preferred_element_type=F32)
        imp = pc[0:tq] + pc[tq:2 * tq] + pc[2 * tq:3 * tq]
        sel = _top_blocks(_block_importance(imp, cur, nb), min(SEL_BLOCKS, nb), nb)
        chosen = jnp.dot(sel.astype(BF16), expand, preferred_element_type=F32)
        sbias3 = _rep3(jnp.where(causal, _mask_bias(chosen), NEG))
        p, den = _softmax_parts(_nt(q3, slc_ref[0, :, ksl]), sbias3, scale)
        o_slc = jnp.dot(p.astype(BF16), slc_ref[0, :, vsl], preferred_element_type=F32) / jnp.maximum(den, 1e-30)
        s = jnp.concatenate([_nt(q3, r[0, :, ksl]) for r in win_refs], axis=1)
        p, den = _softmax_parts(s, wbias3, scale)
        ph = p.astype(BF16)
        o_win, off = 0.0, 0
        for r in win_refs:
            o_win = o_win + jnp.dot(ph[:, off:off + r.shape[1]], r[0, :, vsl], preferred_element_type=F32)
            off += r.shape[1]
        o_win = o_win / jnp.maximum(den, 1e-30)
        for n in range(HPG_C):
            h = g * HPG_C + n
            sl = slice(h * LANES, (h + 1) * LANES)
            rs = slice(n * tq, (n + 1) * tq)
            o = (o_cmp[rs] * gates[:, 3 * h:3 * h + 1] + o_slc[rs] * gates[:, 3 * h + 1:3 * h + 2]
                 + o_win[rs] * gates[:, 3 * h + 2:3 * h + 3])
            o_ref[:, sl] = (o * _silu(z_ref[:, sl])).astype(o_ref.dtype)


def _nsa_prompt_call(u, post, cmpkv, b, t):
    outs = []
    win_h = post["win_h"].reshape(b, t, 512)
    for band, (q0, nqb, klen) in enumerate(_bands(t)):
        rows = functools.partial(_band_rows, b, t, q0)
        ospec, oshape = _band_out(b, nqb, 768)
        span = nqb * PROMPT_TQ
        if span >= WINDOW and q0 * PROMPT_TQ == band * span:
            blocks = [band - 1, band] if band > 0 else [band]
            win_specs = [pl.BlockSpec((1, span, 512), functools.partial(lambda bi, i, k: (bi, k, 0), k=k))
                         for k in blocks]
            win_starts = tuple(k * span for k in blocks)
        else:
            win_specs = [pl.BlockSpec((1, klen, 512), lambda bi, i: (bi, 0, 0))]
            win_starts = (0,)
        outs.append(pl.pallas_call(
            functools.partial(_nsa_prompt_kernel, q0=q0, win_starts=win_starts),
            grid=(b, nqb),
            in_specs=[
                rows(768), rows(768, _OFF["c_z"] // 768), rows(128, _OFF["c_g"] // 128),
                pl.BlockSpec((1, klen // CMP_BLOCK, 512), lambda bi, i: (bi, 0, 0)),
                pl.BlockSpec((1, klen, 512), lambda bi, i: (bi, 0, 0)),
            ] + win_specs,
            out_specs=ospec, out_shape=oshape,
            compiler_params=_params(("parallel", "arbitrary")),
            name="nsa_prompt",
        )(post["qc"], u, u, cmpkv, post["slc_h"].reshape(b, t, 512), *([win_h] * len(win_specs))))
    return jnp.concatenate(outs, axis=1).reshape(b * t, 768)


def _page_specs(n_pages, rows, layer):
    return [pl.BlockSpec((None, None, rows, LANES), functools.partial(_page_index, layer=layer, page=p))
            for p in range(n_pages)]


def _page_index(b, pt_ref, *, layer, page):
    return (layer, pt_ref[b, page], 0, 0)


def _seq_spec(*shape):
    nd = len(shape)
    return pl.BlockSpec((1,) + shape, lambda b, pt: (b,) + (0,) * nd)


def _dsa_score_kernel(pt_ref, qi_ref, wi_ref, kin_ref, *rest):
    del pt_ref
    pages, o_ref = rest[:-1], rest[-1]
    qi = qi_ref[0]
    w = wi_ref[0] * (IDX_HEADS ** -0.5)
    qh = qi.astype(BF16)
    for p, page in enumerate(pages):
        s = jnp.dot(qh, page[...].astype(BF16), preferred_element_type=F32)
        rel = jnp.maximum(s * (IDX_DIM ** -0.5), 0.0) * w
        o_ref[0, :, p * PAGE_SIZE:(p + 1) * PAGE_SIZE] = jnp.sum(rel, axis=0, keepdims=True)
    s_new = jnp.sum(qi * kin_ref[0], axis=-1, keepdims=True)
    sc = jnp.sum(jnp.maximum(s_new * (IDX_DIM ** -0.5), 0.0) * w, axis=0, keepdims=True)
    lane = lax.broadcasted_iota(I32, (1, LANES), 1)
    n_past = len(pages) * PAGE_SIZE
    o_ref[0, :, n_past:n_past + LANES] = jnp.where(lane == 0, sc, -jnp.inf)


def _dsa_score_call(qi, wi, ki_new, cache_kidx_t, page_table, layer):
    bs, n_pages = page_table.shape
    lk = n_pages * PAGE_SIZE + LANES
    grid_spec = pltpu.PrefetchScalarGridSpec(
        num_scalar_prefetch=1, grid=(bs,),
        in_specs=[_seq_spec(IDX_HEADS, IDX_DIM), _seq_spec(IDX_HEADS, 1), _seq_spec(1, IDX_DIM)]
        + _page_specs(n_pages, IDX_DIM, layer),
        out_specs=_seq_spec(1, lk))
    return pl.pallas_call(
        _dsa_score_kernel, grid_spec=grid_spec,
        out_shape=jax.ShapeDtypeStruct((bs, 1, lk), F32),
        compiler_params=_params(("arbitrary",)),
        name="dsa_score",
    )(page_table, qi.reshape(bs, IDX_HEADS, IDX_DIM), wi.reshape(bs, IDX_HEADS, 1),
      ki_new.reshape(bs, 1, IDX_DIM), *([cache_kidx_t] * n_pages))


def _select_kernel(s_ref, o_ref, key_ref, keyt_ref, *, n_valid, n_keys):
    r, l = s_ref.shape
    kpos = lax.broadcasted_iota(I32, (1, l), 1)
    _topk_mask(s_ref[...], jnp.broadcast_to(kpos < n_valid, (r, l)), n_keys, key_ref, keyt_ref, o_ref)


def _select_call(scores, n_valid, n_keys):
    r, l = scores.shape
    return pl.pallas_call(
        functools.partial(_select_kernel, n_valid=n_valid, n_keys=n_keys),
        grid=(1,),
        in_specs=[pl.BlockSpec((r, l), lambda i: (0, 0))],
        out_specs=pl.BlockSpec((r, l), lambda i: (0, 0)),
        out_shape=jax.ShapeDtypeStruct((r, l), F32),
        scratch_shapes=[pltpu.VMEM((r, l), I32), pltpu.VMEM((l, r), I32)],
        compiler_params=_params(("arbitrary",)),
        name="dsa_select",
    )(scores)


def _new_rows(new_ref, slab_col, first, n):
    out = 0.0
    for k in range(n):
        out = out + jnp.where(slab_col == k, new_ref[first + k:first + k + 1, :], 0.0)
    return out


def _decode_attend(q, slab_col, tokf, new_f, pages, new_ref, n_slabs, scale):
    qh = q.astype(BF16)
    s = jnp.concatenate([_nt(qh, pg[...].astype(BF16)) for pg in pages], axis=1)
    col_slab = lax.broadcasted_iota(I32, (1, s.shape[1]), 1) & (n_slabs - 1)
    maskf = jnp.where(col_slab == slab_col, tokf, 0.0)
    s_new = jnp.sum(q * _new_rows(new_ref, slab_col, 0, n_slabs // 2), axis=-1, keepdims=True)
    sm = jnp.where(maskf > 0.0, s, NEG)
    sn = jnp.where(new_f > 0.0, s_new, NEG)
    m = jnp.maximum(jnp.max(sm, axis=-1, keepdims=True), sn)
    c = scale * LOG2E
    p = jnp.exp2((sm - m) * c) * maskf
    p_new = jnp.exp2((sn - m) * c) * new_f
    den = jnp.sum(p, axis=-1, keepdims=True) + p_new
    return p, p_new, den


def _decode_pv(p, p_new, slab_col, pages, new_ref, n_slabs):
    half = n_slabs // 2
    o = p_new * _new_rows(new_ref, slab_col, half, half)
    off = 0
    for pg in pages:
        rows = pg.shape[0]
        pv = jnp.concatenate([pltpu.roll(p[:, off + c * LANES:off + (c + 1) * LANES], half, 1)
                              for c in range(rows // LANES)], axis=1)
        o = o + jnp.dot(pv.astype(BF16), pg[...].astype(BF16), preferred_element_type=F32)
        off += rows
    return o


def _group_of_row(n_rows, heads_per_group):
    row = lax.broadcasted_iota(I32, (n_rows, 1), 0)
    return jnp.where(row >= heads_per_group, 1, 0)


def _dsa_attn_kernel(pt_ref, q_ref, z_ref, m_ref, mnew_ref, new_ref, *rest):
    del pt_ref
    pages, o_ref = rest[:-1], rest[-1]
    new = new_ref.at[0]
    slab_col = _group_of_row(H_A, HPG_A)
    q = q_ref[0]
    p, p_new, den = _decode_attend(q, slab_col, m_ref[0], mnew_ref[0], pages, new, SLABS_A, HEAD_DIM ** -0.5)
    o = _decode_pv(p, p_new, slab_col, pages, new, SLABS_A) / jnp.maximum(den, 1e-30)
    o_ref[0] = (o * _silu(z_ref[0])).astype(o_ref.dtype)


def _dsa_attn_call(q, z, maskf, kv_new, cache_kv, page_table, layer):
    bs, n_pages = page_table.shape
    n_past = n_pages * PAGE_SIZE
    n_ext = n_past * SLABS_A
    mask_rows = jnp.repeat(maskf[:, :n_past], SLABS_A, axis=1).reshape(bs, 1, n_ext)
    mask_new = maskf[:, n_past:n_past + 1].reshape(bs, 1, 1)
    grid_spec = pltpu.PrefetchScalarGridSpec(
        num_scalar_prefetch=1, grid=(bs,),
        in_specs=[_seq_spec(H_A, HEAD_DIM), _seq_spec(H_A, HEAD_DIM), _seq_spec(1, n_ext), _seq_spec(1, 1),
                  _seq_spec(SLABS_A, LANES)]
        + _page_specs(n_pages, PAGE_SIZE * SLABS_A, layer),
        out_specs=_seq_spec(H_A, HEAD_DIM))
    return pl.pallas_call(
        _dsa_attn_kernel, grid_spec=grid_spec,
        out_shape=jax.ShapeDtypeStruct((bs, H_A, HEAD_DIM), BF16),
        compiler_params=_params(("arbitrary",)),
        name="dsa_attn",
    )(page_table, q.reshape(bs, H_A, HEAD_DIM), z.reshape(bs, H_A, HEAD_DIM), mask_rows, mask_new,
      kv_new.reshape(bs, SLABS_A, LANES), *([cache_kv] * n_pages)).reshape(bs, H_A * HEAD_DIM)


def _diff_dec_kernel(pt_ref, q_ref, z_ref, new_ref, lam_ref, g_ref, *rest, layer):
    del pt_ref
    pages, o_ref = rest[:-1], rest[-1]
    new = new_ref.at[0]
    lam, lam_init = _lam_value(lam_ref, layer)
    row = lax.broadcasted_iota(I32, (2 * H_B, 1), 0)
    lo = lax.broadcasted_iota(I32, (1, LANES), 1) < HALF_DIM
    q2 = jnp.where((row < H_B) == lo, q_ref[0], 0.0)
    slab_col = row & (H_B - 1)
    one = jnp.ones((1, 1), F32)
    p, p_new, den = _decode_attend(q2, slab_col, one, one, pages, new, SLABS_B, HALF_DIM ** -0.5)
    inv = 1.0 / jnp.maximum(den, 1e-30)
    p, p_new = p * inv, p_new * inv
    a = p[0:H_B] - lam * p[H_B:]
    a_new = p_new[0:H_B] - lam * p_new[H_B:]
    o = _subln(_decode_pv(a, a_new, slab_col[0:H_B], pages, new, SLABS_B), g_ref[...], lam_init)
    o_ref[0] = (o * _silu(z_ref[0])).astype(o_ref.dtype)


def _diff_dec_call(q, z, kv_new, cache_kv, page_table, lam_l, subln_l, layer):
    bs, n_pages = page_table.shape
    grid_spec = pltpu.PrefetchScalarGridSpec(
        num_scalar_prefetch=1, grid=(bs,),
        in_specs=[_seq_spec(2 * H_B, HEAD_DIM), _seq_spec(H_B, HEAD_DIM), _seq_spec(SLABS_B, LANES),
                  pl.BlockSpec((4, HALF_DIM), lambda b, pt: (0, 0)),
                  pl.BlockSpec((1, HEAD_DIM), lambda b, pt: (0, 0))]
        + _page_specs(n_pages, PAGE_SIZE * SLABS_B, layer),
        out_specs=_seq_spec(H_B, HEAD_DIM))
    qh = q.reshape(bs, H_B, HEAD_DIM)
    return pl.pallas_call(
        functools.partial(_diff_dec_kernel, layer=layer), grid_spec=grid_spec,
        out_shape=jax.ShapeDtypeStruct((bs, H_B, HEAD_DIM), BF16),
        compiler_params=_params(("arbitrary",)),
        name="diff_dec",
    )(page_table, jnp.concatenate([qh, qh], axis=1), z.reshape(bs, H_B, HEAD_DIM),
      kv_new.reshape(bs, SLABS_B, LANES), lam_l, subln_l.reshape(1, HEAD_DIM),
      *([cache_kv] * n_pages)).reshape(bs, H_B * HEAD_DIM)


def _nsa_dec_kernel(*refs, n_pages, has_prev):
    (pt_ref, q_ref, z_ref, gt_ref, slcn_ref, winn_ref, a_ref, pe_ref, a2_ref, win_ref) = refs[:_NSA_DEC_FIXED]
    del pt_ref
    cmp_pages = refs[_NSA_DEC_FIXED:_NSA_DEC_FIXED + n_pages]
    slc_pages = refs[_NSA_DEC_FIXED + n_pages:_NSA_DEC_FIXED + 2 * n_pages]
    o_ref, wout_ref, cmp_scr = refs[-3], refs[-2], refs[-1]
    del has_prev
    n_past = n_pages * PAGE_SIZE
    nb = n_past // CMP_BLOCK
    per_page = PAGE_SIZE // CMP_BLOCK
    scale = HEAD_DIM ** -0.5
    gates = jax.nn.sigmoid(gt_ref[0])
    slc_new = slcn_ref.at[0]
    win_new = winn_ref.at[0]
    wrows = win_ref.shape[0]
    wout_ref[pl.ds(0, wrows - SLABS_C), :] = win_ref[pl.ds(SLABS_C, wrows - SLABS_C), :]
    wout_ref[pl.ds(wrows - SLABS_C, SLABS_C), :] = win_new[...]
    pe_term = jnp.sum(pe_ref[...] * a_ref[...], axis=0, keepdims=True)
    cmp_scr[nb:, :] = jnp.zeros((LANES - nb, 512), F32)
    a2 = a2_ref[...].astype(BF16)
    for p, page in enumerate(cmp_pages):
        sums = jnp.dot(a2, page[...].astype(BF16), preferred_element_type=F32)
        for j in range(per_page):
            bj = p * per_page + j
            for c in range(SLABS_C):
                sl = slice(c * LANES, (c + 1) * LANES)
                cmp_scr[bj:bj + 1, sl] = sums[j * SLABS_C + c:j * SLABS_C + c + 1, :] + pe_term[:, sl]
    cmph = cmp_scr[...].astype(BF16)
    cur = n_past // CMP_BLOCK
    nblk = cur + 1
    lane = lax.broadcasted_iota(I32, (1, LANES), 1)
    cmpf = jnp.where(lane < nb, 1.0, 0.0)
    one = jnp.ones((1, 1), F32)
    q = q_ref[0]
    slab_col = _group_of_row(H_C, HPG_C)
    o_cmp, sels = [], []
    for g in range(KVH_C):
        ksl = slice(g * LANES, (g + 1) * LANES)
        vsl = slice((KVH_C + g) * LANES, (KVH_C + g + 1) * LANES)
        qg = q_ref[0, g * HPG_C:(g + 1) * HPG_C, :]
        p, den = _softmax_parts(_nt(qg.astype(BF16), cmph[:, ksl]), _mask_bias(cmpf), scale)
        pc = p / jnp.maximum(den, 1e-30)
        o_cmp.append(jnp.dot(pc.astype(BF16), cmph[:, vsl], preferred_element_type=F32))
        imp = jnp.sum(pc, axis=0, keepdims=True)
        sels.append(_top_blocks(_block_importance(imp, cur, nblk), min(SEL_BLOCKS, nblk), nblk))
    sel = jnp.where(slab_col == 0, sels[0], sels[1])
    prow = lax.broadcasted_iota(I32, (1, PAGE_SIZE * SLABS_C), 1) // (CMP_BLOCK * SLABS_C)
    tok = []
    for p in range(n_pages):
        m = jnp.zeros((H_C, PAGE_SIZE * SLABS_C), F32)
        for j in range(per_page):
            bj = p * per_page + j
            m = jnp.where(prow == j, sel[:, bj:bj + 1], m)
        tok.append(m)
    slcf = jnp.concatenate(tok, axis=1)
    p, p_new, den = _decode_attend(q, slab_col, slcf, sel[:, cur:cur + 1], slc_pages, slc_new, SLABS_C, scale)
    o_slc = _decode_pv(p, p_new, slab_col, slc_pages, slc_new, SLABS_C) / jnp.maximum(den, 1e-30)
    wtok = wrows // SLABS_C
    wrow = lax.broadcasted_iota(I32, (1, wrows), 1)
    winf = jnp.where(wrow >= (wtok + 1 - WINDOW) * SLABS_C, 1.0, 0.0)
    p, p_new, den = _decode_attend(q, slab_col, winf, one, [win_ref], win_new, SLABS_C, scale)
    o_win = _decode_pv(p, p_new, slab_col, [win_ref], win_new, SLABS_C) / jnp.maximum(den, 1e-30)
    for h in range(H_C):
        g, n = divmod(h, HPG_C)
        o = (o_cmp[g][n:n + 1] * gates[:, 3 * h:3 * h + 1] + o_slc[h:h + 1] * gates[:, 3 * h + 1:3 * h + 2]
             + o_win[h:h + 1] * gates[:, 3 * h + 2:3 * h + 3])
        o_ref[0, h:h + 1, :] = (o * _silu(z_ref[0, h:h + 1, :])).astype(o_ref.dtype)


_NSA_DEC_FIXED = 10


def _page_block_weights(alpha):
    per_page = PAGE_SIZE // CMP_BLOCK
    a = alpha.reshape(CMP_BLOCK, SLABS_C)
    eye_c = jnp.eye(SLABS_C, dtype=alpha.dtype)
    eye_j = jnp.eye(per_page, dtype=alpha.dtype)
    w = eye_j[:, None, :, None, None] * eye_c[None, :, None, None, :] * a.T[None, :, None, :, None]
    return w.reshape(per_page * SLABS_C, PAGE_SIZE * SLABS_C)


def _nsa_dec_call(q, z, gt, slc_new, win_new, alpha_l, alpha_e, pe_e, win_state, cache_cmp, cache_slc, page_table,
                  layer, win_prev):
    bs, n_pages = page_table.shape
    depth, _, wrows, _ = win_state.shape
    a2 = _page_block_weights(alpha_l)
    in_specs = [_seq_spec(H_C, HEAD_DIM), _seq_spec(H_C, HEAD_DIM), _seq_spec(1, LANES),
                _seq_spec(SLABS_C, LANES), _seq_spec(SLABS_C, LANES),
                pl.BlockSpec((CMP_BLOCK, 512), lambda b, pt: (0, 0)),
                pl.BlockSpec((CMP_BLOCK, 512), lambda b, pt: (0, 0)),
                pl.BlockSpec(a2.shape, lambda b, pt: (0, 0)),
                pl.BlockSpec((None, None, wrows, LANES), lambda b, pt: (layer, b, 0, 0))]
    assert len(in_specs) + 1 == _NSA_DEC_FIXED
    in_specs += _page_specs(n_pages, PAGE_SIZE * SLABS_C, layer) + _page_specs(n_pages, PAGE_SIZE * SLABS_C, layer)
    args = [page_table, q.reshape(bs, H_C, HEAD_DIM), z.reshape(bs, H_C, HEAD_DIM), gt.reshape(bs, 1, LANES),
            slc_new.reshape(bs, SLABS_C, LANES), win_new.reshape(bs, SLABS_C, LANES), alpha_e, pe_e, a2, win_state]
    args += [cache_cmp] * n_pages + [cache_slc] * n_pages
    aliases = {}
    if win_prev is not None:
        aliases[len(args)] = 1
        args.append(win_prev)
        in_specs.append(pl.BlockSpec(memory_space=pl.ANY))
    grid_spec = pltpu.PrefetchScalarGridSpec(
        num_scalar_prefetch=1, grid=(bs,),
        in_specs=in_specs,
        out_specs=[_seq_spec(H_C, HEAD_DIM),
                   pl.BlockSpec((None, None, wrows, LANES), lambda b, pt: (layer, b, 0, 0))],
        scratch_shapes=[pltpu.VMEM((LANES, 512), F32)])

    def body(*refs):
        if win_prev is not None:
            refs = refs[:_NSA_DEC_FIXED + 2 * n_pages] + refs[_NSA_DEC_FIXED + 1 + 2 * n_pages:]
        _nsa_dec_kernel(*refs, n_pages=n_pages, has_prev=win_prev is not None)

    y, win_out = pl.pallas_call(
        body, grid_spec=grid_spec,
        out_shape=[jax.ShapeDtypeStruct((bs, H_C, HEAD_DIM), BF16),
                   jax.ShapeDtypeStruct((depth, bs, wrows, LANES), F32)],
        input_output_aliases=aliases,
        compiler_params=_params(("arbitrary",)),
        name="nsa_dec",
    )(*args)
    return y.reshape(bs, H_C * HEAD_DIM), win_out


def _outproj_kernel(ma_ref, mb_ref, mc_ref, w_ref, x_ref, gate_ref, fg_ref, o_ref, *, final):
    wa, wb = H_A * HEAD_DIM, (H_A + H_B) * HEAD_DIM
    out = jnp.dot(ma_ref[...], w_ref[0:wa, :], preferred_element_type=F32)
    out = out + jnp.dot(mb_ref[...], w_ref[wa:wb, :], preferred_element_type=F32)
    out = out + jnp.dot(mc_ref[...], w_ref[wb:, :], preferred_element_type=F32)
    xn = x_ref[...] + gate_ref[...] * out
    if final:
        xn = xn * lax.rsqrt(jnp.mean(xn * xn, axis=-1, keepdims=True) + NORM_EPS) * fg_ref[...]
    o_ref[...] = xn


def _outproj_call(ma, mb, mc, w, x, gate, fg, rows_per_batch, final):
    m, d = x.shape
    tm = min(m, 256)
    if rows_per_batch > 1:
        per = rows_per_batch // tm
        gspec = pl.BlockSpec((None, 1, d), lambda i: (i // per, 0, 0))
    else:
        gspec = pl.BlockSpec((tm, d), lambda i: (i, 0))

    def rows(width):
        return pl.BlockSpec((tm, width), lambda i: (i, 0))

    return pl.pallas_call(
        functools.partial(_outproj_kernel, final=final),
        grid=(m // tm,),
        in_specs=[rows(768), rows(512), rows(768), pl.BlockSpec((d, d), lambda i: (0, 0)), rows(d), gspec,
                  pl.BlockSpec((1, d), lambda i: (0, 0))],
        out_specs=rows(d),
        out_shape=jax.ShapeDtypeStruct((m, d), F32),
        compiler_params=_params(("parallel",)),
        name="outproj",
    )(ma, mb, mc, w, x, gate, fg)


_SRC = {}
_o = 0
for _n, _w in _IN_SPLITS:
    _SRC[_n] = (_o, _w)
    _o += _w
IN_WIDTH = _o


def _pack_kernel(w_ref, o_ref):
    rows = w_ref.shape[0]
    for name, width in _PAD_ORDER:
        dst = _OFF[name]
        sw = 0
        if name in _SRC:
            s, sw = _SRC[name]
            o_ref[:, dst:dst + sw] = w_ref[:, s:s + sw].astype(BF16)
        if width > sw:
            o_ref[:, dst + sw:dst + width] = jnp.zeros((rows, width - sw), BF16)


def _pad_w_in(w_in):
    depth, d, width = w_in.shape
    assert width == IN_WIDTH
    tr = 256
    return pl.pallas_call(
        _pack_kernel,
        grid=(depth, d // tr),
        in_specs=[pl.BlockSpec((None, tr, width), lambda l, i: (l, i, 0))],
        out_specs=pl.BlockSpec((None, tr, NP), lambda l, i: (l, i, 0)),
        out_shape=jax.ShapeDtypeStruct((depth, d, NP), BF16),
        compiler_params=_params(("parallel", "parallel")),
        name="pack_w_in",
    )(w_in)


def _ucol(u, name, width=None):
    width = width or dict(_PAD_ORDER)[name]
    return u[:, _OFF[name]:_OFF[name] + width]


def kernel(x_prompt, x_sample, cache_a_kv, cache_a_kidx, cache_b_kv, cache_c_cmp_kv, cache_c_slc_kv,
           state_c_win_kv, page_table, c_prompt, c_sample, w_mod, b_mod, norm_g, w_in, w_out, lam,
           subln_g, cmp_alpha, cmp_pe, final_g):
    depth, d, _ = w_mod.shape
    b, t, _ = x_prompt.shape
    bs, ts, _ = x_sample.shape
    assert ts == 1 and t % 256 == 0
    n_pool = cache_a_kv.shape[1]
    n_pages = page_table.shape[1]
    n_past = n_pages * PAGE_SIZE
    wbuf = state_c_win_kv.shape[2]
    assert wbuf == WINDOW and n_past >= WINDOW

    w_in_p = _pad_w_in(w_in.astype(BF16))
    w_out_h = w_out.astype(BF16)
    mod = _mod_call(jnp.concatenate([c_prompt, c_sample], axis=0), w_mod, b_mod)
    alpha_e = jnp.broadcast_to(cmp_alpha[..., None], cmp_alpha.shape + (HEAD_DIM,)).reshape(depth, CMP_BLOCK, 512)
    pe_e = cmp_pe.reshape(depth, CMP_BLOCK, 512)
    fg = final_g.reshape(1, d)

    ca_kv = cache_a_kv.reshape(depth, n_pool, PAGE_SIZE * SLABS_A, LANES)
    cb_kv = cache_b_kv.reshape(depth, n_pool, PAGE_SIZE * SLABS_B, LANES)
    cc_cmp = cache_c_cmp_kv.reshape(depth, n_pool, PAGE_SIZE * SLABS_C, LANES)
    cc_slc = cache_c_slc_kv.reshape(depth, n_pool, PAGE_SIZE * SLABS_C, LANES)
    win_state = state_c_win_kv.reshape(depth, bs, wbuf * SLABS_C, LANES)
    ca_kidx_t = jnp.swapaxes(cache_a_kidx, 2, 3)

    tabs_p = _rope_tables(jnp.arange(t, dtype=I32))
    tabs_s = _rope_tables(n_past + jnp.arange(1, dtype=I32))

    x = x_prompt.reshape(b * t, d)
    post = None
    for layer in range(depth):
        shift, scale, gate = (mod[layer, :b, i * d:(i + 1) * d].reshape(b, 1, d) for i in range(3))
        u = _inproj_call(x, norm_g[layer].reshape(1, d), scale, shift, w_in_p, layer, t)
        post = _post_call(u, tabs_p, t, layer, depth, post)
        ya = _dsa_prompt_call(u, post, b, t)
        yb = _diff_prompt_call(u, post, lam[layer], subln_g[layer], layer, b, t)
        cmpkv = _compress_call(post["cmp"], alpha_e[layer], pe_e[layer], layer, b, t)
        yc = _nsa_prompt_call(u, post, cmpkv, b, t)
        x = _outproj_call(ya, yb, yc, w_out_h[layer], x, gate, fg, t, layer == depth - 1)
    y_prompt = x.reshape(b, t, d)
    st_p = post

    x = x_sample.reshape(bs, d)
    post = None
    win_out = None
    for layer in range(depth):
        shift, scale, gate = (mod[layer, b:, i * d:(i + 1) * d] for i in range(3))
        u = _inproj_call(x, norm_g[layer].reshape(1, d), scale, shift, w_in_p, layer, 1)
        post = _post_call(u, tabs_s, 1, layer, depth, post)
        n_keys = min(TOPK_MAX, (n_past + 1) // 4)
        scores = _dsa_score_call(post["qi"], _ucol(u, "a_wi", IDX_HEADS), post["a_kidx"][layer], ca_kidx_t,
                                 page_table, layer)
        maskf = _select_call(scores.reshape(bs, -1), n_past + 1, n_keys)
        ya = _dsa_attn_call(post["qa"], _ucol(u, "a_z"), maskf, post["a_kv"][layer], ca_kv, page_table, layer)
        yb = _diff_dec_call(post["qb"], _ucol(u, "b_z"), post["b_kv"][layer], cb_kv, page_table, lam[layer],
                            subln_g[layer], layer)
        yc, win_out = _nsa_dec_call(post["qc"], _ucol(u, "c_z"), _ucol(u, "c_g"), post["slc"][layer],
                                    post["win"][layer], cmp_alpha[layer], alpha_e[layer], pe_e[layer], win_state,
                                    cc_cmp, cc_slc,
                                    page_table, layer, win_out)
        x = _outproj_call(ya, yb, yc, w_out_h[layer], x, gate, fg, 1, layer == depth - 1)
    y_sample = x.reshape(bs, 1, d)
    st_s = post

    outs = []
    for name, tail in (("a_kv", (2, KVH_A, HEAD_DIM)), ("a_kidx", (IDX_DIM,)), ("b_kv", (2, H_B, HEAD_DIM)),
                       ("cmp", (2, KVH_C, HEAD_DIM)), ("slc", (2, KVH_C, HEAD_DIM))):
        outs.append(st_p[name].reshape((depth, b, t) + tail))
        outs.append(st_s[name].reshape((depth, bs, 1) + tail))
    keep = min(WINDOW, t)
    win_p = st_p["win"].reshape(depth, b, t, 2, KVH_C, HEAD_DIM)[:, :, t - keep:]
    win_s = win_out.reshape(depth, bs, wbuf, 2, KVH_C, HEAD_DIM)
    return (y_prompt, y_sample, *outs, win_p, win_s)
```
